```python
import math
import jax
import jax.numpy as jnp
from jax import lax
import numpy as np

D_MODEL = 2048
BATCH = 4
SEQ = 2048
DEPTH = 4
DEC_BATCH = 8
DEC_SEQ = 8
PAST_LEN = 16384
PAGE_SIZE = 128

N_REC = (DEPTH + 1) // 2
N_ATTN = DEPTH // 2
M_HEADS = 4
M_QK_DIM = D_MODEL // 16
M_V_DIM = D_MODEL // 8
M_CHUNK = 64
CONV_CH = D_MODEL // 2
CONV_WIDTH = 31
REC_MIX_W = M_HEADS * M_V_DIM + CONV_CH
REC_SPLITS = (M_HEADS * M_QK_DIM, M_HEADS * M_QK_DIM, M_HEADS * M_V_DIM,
              M_HEADS, M_HEADS, M_HEADS * M_V_DIM, CONV_CH, CONV_CH)
REC_IN_W = sum(REC_SPLITS)
A_HEADS = 16
A_HEAD_DIM = D_MODEL // A_HEADS
MOBA_BLOCK = 256
MOBA_TOPK = 3
MOBA_Q_BLOCK = 32
N_BUCKETS = 32
T5_MAX_DIST = 128
N_EXPERTS = 16
N_GROUPS = 4
EXP_PER_GROUP = N_EXPERTS // N_GROUPS
TOP_K = 2
D_EXPERT = D_MODEL // 2
MOE_ROWS = 128
DEEPNORM_ALPHA = (2 * DEPTH) ** 0.25
DEEPNORM_BETA = (8 * DEPTH) ** -0.25
LN_EPS = 1e-5
NEG_INF = -1e30

kernel_name = "hybrid_mlstm_conformer_moba_moe_step"


def layer_norm(x, g, b):
    xf = x.astype(jnp.float32)
    mu = jnp.mean(xf, -1, keepdims=True)
    var = jnp.mean(jnp.square(xf - mu), -1, keepdims=True)
    y = (xf - mu) * lax.rsqrt(var + LN_EPS) * g.astype(jnp.float32) + b.astype(jnp.float32)
    return y.astype(x.dtype)


def mlstm_chunkwise(q, k, v, logi, logf, C0, n0, m0):
    f32 = jnp.float32
    B, S, H, DK = q.shape
    DV = v.shape[-1]
    L = math.gcd(S, M_CHUNK)
    NC = S // L

    def chunks(a):
        a = a.astype(f32).reshape((B, NC, L, H) + a.shape[3:])
        return jnp.moveaxis(jnp.moveaxis(a, 1, 0), 3, 2)

    causal = jnp.tril(jnp.ones((L, L), bool))

    def step(carry, inp):
        C, n, m = carry
        qq, kk, vv, li, lf = inp
        b = jnp.cumsum(lf, axis=-1)
        dm = jnp.where(causal, b[..., :, None] - b[..., None, :] + li[..., None, :], NEG_INF)
        inter = b + m[..., None]
        mt = jnp.maximum(inter, jnp.max(dm, -1))
        w_int = jnp.exp(inter - mt)
        a_ts = jnp.exp(dm - mt[..., None]) * jnp.einsum("bhtd,bhsd->bhts", qq, kk)
        num = w_int[..., None] * jnp.einsum("bhvd,bhtd->bhtv", C, qq) + jnp.einsum("bhts,bhsv->bhtv", a_ts, vv)
        den = w_int * jnp.einsum("bhd,bhtd->bht", n, qq) + jnp.sum(a_ts, -1)
        h = num / jnp.maximum(jnp.abs(den), jnp.exp(-mt))[..., None]
        g = b[..., -1]
        dec = g[..., None] - b + li
        m_new = jnp.maximum(g + m, jnp.max(dec, -1))
        a_old = jnp.exp(g + m - m_new)
        wk = jnp.exp(dec - m_new[..., None])
        C_new = a_old[..., None, None] * C + jnp.einsum("bhs,bhsv,bhsd->bhvd", wk, vv, kk)
        n_new = a_old[..., None] * n + jnp.einsum("bhs,bhsd->bhd", wk, kk)
        return (C_new, n_new, m_new), h

    (C, n, m), hs = lax.scan(step, (C0.astype(f32), n0.astype(f32), m0.astype(f32)),
                             (chunks(q), chunks(k), chunks(v), chunks(logi), chunks(logf)))
    h = jnp.moveaxis(jnp.moveaxis(hs, 2, 3), 0, 1).reshape(B, S, H, DV)
    return h, C, n, m


def rec_mixer(x, C0, n0, m0, buf0, w_in, b_i, b_f, g_h, w_dw, b_dw, g_cn, b_cn, w_out):
    f32 = jnp.float32
    B, S, _ = x.shape
    u = x @ w_in
    q, k, v, ig, fg, og, ga, gb = jnp.split(u, np.cumsum(REC_SPLITS)[:-1].tolist(), axis=-1)
    q = q.reshape(B, S, M_HEADS, M_QK_DIM) * (M_QK_DIM ** -0.5)
    k = k.reshape(B, S, M_HEADS, M_QK_DIM)
    v = v.reshape(B, S, M_HEADS, M_V_DIM)
    logi = ig.astype(f32) + b_i.astype(f32)
    logf = jax.nn.log_sigmoid(fg.astype(f32) + b_f.astype(f32))
    h, C, n, m = mlstm_chunkwise(q, k, v, logi, logf, C0, n0, m0)
    mu = jnp.mean(h, -1, keepdims=True)
    var = jnp.mean(jnp.square(h - mu), -1, keepdims=True)
    h = ((h - mu) * lax.rsqrt(var + LN_EPS)).reshape(B, S, M_HEADS * M_V_DIM) * g_h.astype(f32)
    h_a = (h * jax.nn.sigmoid(og.astype(f32))).astype(x.dtype)
    glu = ga * jax.nn.sigmoid(gb)
    conv_in = jnp.concatenate([buf0.astype(x.dtype), glu], axis=1)
    new_buf = conv_in[:, conv_in.shape[1] - (CONV_WIDTH - 1):]
    c = lax.conv_general_dilated(conv_in, w_dw[:, None, :].astype(x.dtype), (1,), "VALID",
                                 dimension_numbers=("NWC", "WIO", "NWC"),
                                 feature_group_count=CONV_CH) + b_dw
    c = jax.nn.silu(layer_norm(c, g_cn, b_cn))
    out = jnp.concatenate([h_a, c.astype(x.dtype)], axis=-1) @ w_out
    return out, (C, n, m, new_buf)


def t5_bucket(dist):
    exact = N_BUCKETS // 2
    n = jnp.maximum(dist, 0)
    nf = jnp.maximum(n, 1).astype(jnp.float32)
    large = exact + (jnp.log(nf / exact) / math.log(T5_MAX_DIST / exact)
                     * (N_BUCKETS - exact)).astype(jnp.int32)
    return jnp.where(n < exact, n, jnp.minimum(large, N_BUCKETS - 1))


def moba_attend(q, q_pos, k_own, v_own, own_pos, bias_t,
                k_sel=None, v_sel=None, sel_pos=None, sel_valid=None):
    f32 = jnp.float32
    H, DH = q.shape[1], q.shape[2]
    scale = DH ** -0.5
    qf = q.astype(f32)
    d_own = q_pos[:, None] - own_pos[None, :]
    l_own = (jnp.einsum("qhd,khd->qhk", qf, k_own.astype(f32)) * scale
             + jnp.transpose(bias_t[:, t5_bucket(d_own)], (1, 0, 2)))
    l_own = jnp.where((d_own >= 0)[:, None, :], l_own, NEG_INF)
    if k_sel is None:
        p = jax.nn.softmax(l_own, axis=-1)
        return jnp.einsum("qhk,khd->qhd", p, v_own.astype(f32))
    hidx = jnp.arange(H)
    l_sel = (jnp.einsum("qhd,qhkd->qhk", qf, k_sel.astype(f32)) * scale
             + bias_t[hidx[None, :, None], t5_bucket(q_pos[:, None, None] - sel_pos)])
    l_sel = jnp.where(sel_valid, l_sel, NEG_INF)
    ks = l_sel.shape[-1]
    p = jax.nn.softmax(jnp.concatenate([l_sel, l_own], axis=-1), axis=-1)
    return (jnp.einsum("qhk,qhkd->qhd", p[..., :ks], v_sel.astype(f32))
            + jnp.einsum("qhk,khd->qhd", p[..., ks:], v_own.astype(f32)))


def moba_prompt(q, k, v, bias_t):
    f32 = jnp.float32
    B, S, H, DH = q.shape
    nb = -(-S // MOBA_BLOCK)
    pad = ((0, 0), (0, nb * MOBA_BLOCK - S), (0, 0), (0, 0))
    kp = jnp.pad(k, pad)
    vp = jnp.pad(v, pad)
    k_blk = kp.reshape(B, nb, MOBA_BLOCK, H, DH).transpose(0, 3, 1, 2, 4)
    v_blk = vp.reshape(B, nb, MOBA_BLOCK, H, DH).transpose(0, 3, 1, 2, 4)
    means = jnp.mean(k_blk.astype(f32), axis=3)
    pos = jnp.arange(S)
    scores = jnp.einsum("bshd,bhnd->bshn", q.astype(f32), means)
    past = jnp.arange(nb)[None, :] < (pos // MOBA_BLOCK)[:, None]
    scores = jnp.where(past[None, :, None, :], scores, NEG_INF)
    ksel = min(MOBA_TOPK, nb)
    sel_score, sel_idx = lax.top_k(scores, ksel)
    sel_valid = sel_score > NEG_INF * 0.5
    nq = S // MOBA_Q_BLOCK
    hidx = jnp.arange(H)
    offs = jnp.arange(MOBA_BLOCK)

    def step(i):
        b = i // nq
        q0 = (i % nq) * MOBA_Q_BLOCK
        qc = lax.dynamic_slice(q, (b, q0, 0, 0), (1, MOBA_Q_BLOCK, H, DH))[0]
        idx = lax.dynamic_slice(sel_idx, (b, q0, 0, 0), (1, MOBA_Q_BLOCK, H, ksel))[0]
        val = lax.dynamic_slice(sel_valid, (b, q0, 0, 0), (1, MOBA_Q_BLOCK, H, ksel))[0]
        k_sel = k_blk[b, hidx[None, :, None], idx].reshape(MOBA_Q_BLOCK, H, ksel * MOBA_BLOCK, DH)
        v_sel = v_blk[b, hidx[None, :, None], idx].reshape(MOBA_Q_BLOCK, H, ksel * MOBA_BLOCK, DH)
        sel_pos = (idx[..., None] * MOBA_BLOCK + offs).reshape(MOBA_Q_BLOCK, H, ksel * MOBA_BLOCK)
        sel_val = jnp.broadcast_to(val[..., None], (MOBA_Q_BLOCK, H, ksel, MOBA_BLOCK)
                                   ).reshape(MOBA_Q_BLOCK, H, ksel * MOBA_BLOCK)
        blk0 = (q0 // MOBA_BLOCK) * MOBA_BLOCK
        k_own = lax.dynamic_slice(kp, (b, blk0, 0, 0), (1, MOBA_BLOCK, H, DH))[0]
        v_own = lax.dynamic_slice(vp, (b, blk0, 0, 0), (1, MOBA_BLOCK, H, DH))[0]
        return moba_attend(qc, q0 + jnp.arange(MOBA_Q_BLOCK), k_own, v_own, blk0 + offs, bias_t,
                           k_sel, v_sel, sel_pos, sel_val)

    out = lax.map(step, jnp.arange(B * nq))
    return out.reshape(B, S, H, DH)


def moba_sample(q, k_new, v_new, cache_k, cache_v, layer, page_table, bias_t):
    f32 = jnp.float32
    DB, Sn, H, DH = q.shape
    P = page_table.shape[1] * PAGE_SIZE
    ppb = MOBA_BLOCK // PAGE_SIZE
    n_full = P // MOBA_BLOCK
    n_part = (P % MOBA_BLOCK) // PAGE_SIZE
    assert (P % MOBA_BLOCK) + Sn <= MOBA_BLOCK
    q_pos = P + jnp.arange(Sn)
    hidx = jnp.arange(H)
    offs = jnp.arange(MOBA_BLOCK)
    own_k, own_v = k_new, v_new
    if n_part > 0:
        pp = page_table[:, n_full * ppb:n_full * ppb + n_part]
        pk = cache_k[pp, layer].transpose(0, 1, 3, 2, 4).reshape(DB, n_part * PAGE_SIZE, H, DH)
        pv = cache_v[pp, layer].transpose(0, 1, 3, 2, 4).reshape(DB, n_part * PAGE_SIZE, H, DH)
        own_k = jnp.concatenate([pk.astype(k_new.dtype), k_new], axis=1)
        own_v = jnp.concatenate([pv.astype(v_new.dtype), v_new], axis=1)
    own_pos = n_full * MOBA_BLOCK + jnp.arange(own_k.shape[1])
    if n_full > 0:
        page_sum = jnp.sum(cache_k[:, layer].astype(f32), axis=2)
        blk_mean = page_sum[page_table[:, :n_full * ppb]].reshape(DB, n_full, ppb, H, DH).sum(2) / MOBA_BLOCK
        scores = jnp.einsum("bshd,bnhd->bshn", q.astype(f32), blk_mean)
        ksel = min(MOBA_TOPK, n_full)
        _, sel = lax.top_k(scores, ksel)
        pages = page_table[jnp.arange(DB)[:, None, None, None, None],
                           sel[..., None] * ppb + jnp.arange(ppb)]

    def per_seq(b):
        if n_full == 0:
            return moba_attend(q[b], q_pos, own_k[b], own_v[b], own_pos, bias_t)
        pg = pages[b]
        hsel = hidx[None, :, None, None]
        k_sel = cache_k[pg, layer, hsel].reshape(Sn, H, ksel * MOBA_BLOCK, DH)
        v_sel = cache_v[pg, layer, hsel].reshape(Sn, H, ksel * MOBA_BLOCK, DH)
        sel_pos = (sel[b][..., None] * MOBA_BLOCK + offs).reshape(Sn, H, ksel * MOBA_BLOCK)
        sel_val = jnp.ones(sel_pos.shape, bool)
        return moba_attend(q[b], q_pos, own_k[b], own_v[b], own_pos, bias_t,
                           k_sel, v_sel, sel_pos, sel_val)

    return lax.map(per_seq, jnp.arange(DB))


def split_qkv(x, w_qkv):
    B, S, _ = x.shape
    qkv = (x @ w_qkv).reshape(B, S, 3, A_HEADS, A_HEAD_DIM)
    return qkv[:, :, 0], qkv[:, :, 1], qkv[:, :, 2]


def attn_prompt(x, w_qkv, w_o, bias_t):
    B, S, _ = x.shape
    q, k, v = split_qkv(x, w_qkv)
    o = moba_prompt(q, k, v, bias_t).astype(x.dtype).reshape(B, S, D_MODEL)
    return o @ w_o, k.transpose(0, 2, 1, 3), v.transpose(0, 2, 1, 3)


def attn_sample(x, cache_k, cache_v, layer, page_table, w_qkv, w_o, bias_t):
    B, S, _ = x.shape
    q, k, v = split_qkv(x, w_qkv)
    o = moba_sample(q, k, v, cache_k, cache_v, layer, page_table, bias_t).astype(x.dtype).reshape(B, S, D_MODEL)
    return o @ w_o, k.transpose(0, 2, 1, 3), v.transpose(0, 2, 1, 3)


def moe(x, w_router, b_router, w1, w3, w2):
    f32 = jnp.float32
    shp = x.shape
    xt = x.reshape(-1, D_MODEL)
    T = xt.shape[0]
    aff = jax.nn.softmax((xt @ w_router).astype(f32) + b_router.astype(f32), axis=-1)
    aff_g = aff.reshape(T, N_GROUPS, EXP_PER_GROUP)
    g_sel = jnp.argmax(jnp.sum(lax.top_k(aff_g, TOP_K)[0], -1), -1)
    in_g = jnp.take_along_axis(aff_g, g_sel[:, None, None], axis=1)[:, 0]
    top_v, top_i = lax.top_k(in_g, TOP_K)
    e_idx = (g_sel[:, None] * EXP_PER_GROUP + top_i).reshape(-1)
    gates = (top_v / jnp.sum(top_v, -1, keepdims=True)).reshape(-1)
    tok = jnp.repeat(jnp.arange(T), TOP_K)
    order = jnp.argsort(e_idx, stable=True)
    e_s, tok_s, gate_s = e_idx[order], tok[order], gates[order]
    A = T * TOP_K
    counts = jnp.zeros((N_EXPERTS,), jnp.int32).at[e_idx].add(1)
    padded = (counts + MOE_ROWS - 1) // MOE_ROWS * MOE_ROWS
    pad_end = jnp.cumsum(padded)
    pad_start = pad_end - padded
    start = jnp.cumsum(counts) - counts
    dest = pad_start[e_s] + jnp.arange(A) - start[e_s]
    nblk = -(-(A + N_EXPERTS * (MOE_ROWS - 1)) // MOE_ROWS)
    xbuf = jnp.zeros((nblk * MOE_ROWS, D_MODEL), x.dtype).at[dest].set(xt[tok_s])
    blk_e = jnp.minimum(jnp.searchsorted(pad_end, jnp.arange(nblk) * MOE_ROWS, side="right"), N_EXPERTS - 1)

    def expert_block(args):
        xb, e = args
        hb = jax.nn.silu(xb @ w1[e]) * (xb @ w3[e])
        return hb @ w2[e]

    ybuf = lax.map(expert_block, (xbuf.reshape(nblk, MOE_ROWS, D_MODEL), blk_e)).reshape(-1, D_MODEL)
    y = jnp.zeros((T, D_MODEL), f32).at[tok_s].add(ybuf[dest].astype(f32) * gate_s[:, None])
    return y.astype(x.dtype).reshape(shp)


def setup_inputs(seed: int = 0) -> dict:
    key = jax.random.key(seed)
    keys = iter(jax.random.split(key, 40))
    f32 = jnp.float32

    def nrm(shape, scale):
        return jax.random.normal(next(keys), shape, f32) * scale

    n_pages = PAST_LEN // PAGE_SIZE
    n_used = DEC_BATCH * n_pages
    n_pool = (5 * n_used + 3) // 4
    page_table = jax.random.permutation(next(keys), n_pool)[:n_used].reshape(DEC_BATCH, n_pages).astype(jnp.int32)
    beta = DEEPNORM_BETA
    return {
        "x_prompt": nrm((BATCH, SEQ, D_MODEL), 1.0),
        "x_sample": nrm((DEC_BATCH, DEC_SEQ, D_MODEL), 1.0),
        "state_mlstm_C": nrm((N_REC, DEC_BATCH, M_HEADS, M_V_DIM, M_QK_DIM), 0.5),
        "state_mlstm_n": nrm((N_REC, DEC_BATCH, M_HEADS, M_QK_DIM), 0.5),
        "state_mlstm_m": nrm((N_REC, DEC_BATCH, M_HEADS), 1.0),
        "state_conv": nrm((N_REC, DEC_BATCH, CONV_WIDTH - 1, CONV_CH), 0.5),
        "cache_k": nrm((n_pool, N_ATTN, A_HEADS, PAGE_SIZE, A_HEAD_DIM), 1.0),
        "cache_v": nrm((n_pool, N_ATTN, A_HEADS, PAGE_SIZE, A_HEAD_DIM), 1.0),
        "page_table": page_table,
        "w_in_rec": nrm((N_REC, D_MODEL, REC_IN_W), D_MODEL ** -0.5),
        "b_gate_i": nrm((N_REC, M_HEADS), 0.1),
        "b_gate_f": 3.0 + 3.0 * jax.random.uniform(next(keys), (N_REC, M_HEADS), f32),
        "g_mlstm_norm": 1.0 + nrm((N_REC, M_HEADS * M_V_DIM), 0.02),
        "w_dw": nrm((N_REC, CONV_WIDTH, CONV_CH), CONV_WIDTH ** -0.5),
        "b_dw": nrm((N_REC, CONV_CH), 0.02),
        "g_conv_norm": 1.0 + nrm((N_REC, CONV_CH), 0.02),
        "b_conv_norm": nrm((N_REC, CONV_CH), 0.02),
        "w_out_rec": nrm((N_REC, REC_MIX_W, D_MODEL), REC_MIX_W ** -0.5 * beta),
        "w_qkv_attn": nrm((N_ATTN, D_MODEL, 3 * D_MODEL), D_MODEL ** -0.5),
        "w_o_attn": nrm((N_ATTN, D_MODEL, D_MODEL), D_MODEL ** -0.5 * beta),
        "rel_bias": nrm((N_BUCKETS, A_HEADS), 0.5),
        "ln_mix_g": 1.0 + nrm((DEPTH, D_MODEL), 0.02),
        "ln_mix_b": nrm((DEPTH, D_MODEL), 0.02),
        "ln_ffn_g": 1.0 + nrm((DEPTH, D_MODEL), 0.02),
        "ln_ffn_b": nrm((DEPTH, D_MODEL), 0.02),
        "w_router": nrm((D_MODEL, N_EXPERTS), D_MODEL ** -0.5),
        "b_router": nrm((N_EXPERTS,), 0.01),
        "w1_exp": nrm((DEPTH, N_EXPERTS, D_MODEL, D_EXPERT), D_MODEL ** -0.5),
        "w3_exp": nrm((DEPTH, N_EXPERTS, D_MODEL, D_EXPERT), D_MODEL ** -0.5),
        "w2_exp": nrm((DEPTH, N_EXPERTS, D_EXPERT, D_MODEL), D_EXPERT ** -0.5 * beta),
    }


def reference(x_prompt, x_sample, state_mlstm_C, state_mlstm_n, state_mlstm_m, state_conv,
              cache_k, cache_v, page_table, w_in_rec, b_gate_i, b_gate_f, g_mlstm_norm,
              w_dw, b_dw, g_conv_norm, b_conv_norm, w_out_rec, w_qkv_attn, w_o_attn, rel_bias,
              ln_mix_g, ln_mix_b, ln_ffn_g, ln_ffn_b, w_router, b_router, w1_exp, w3_exp, w2_exp):
    f32 = jnp.float32
    bias_t = rel_bias.T
    bp = x_prompt.shape[0]
    xp, xs = x_prompt, x_sample
    zC = jnp.zeros((bp, M_HEADS, M_V_DIM, M_QK_DIM), f32)
    zn = jnp.zeros((bp, M_HEADS, M_QK_DIM), f32)
    zm = jnp.zeros((bp, M_HEADS), f32)
    zbuf = jnp.zeros((bp, CONV_WIDTH - 1, CONV_CH), x_prompt.dtype)
    rec_p, rec_s, kv_p, kv_s = [], [], [], []
    for l in range(DEPTH):
        if l % 2 == 0:
            r = l // 2
            wr = (w_in_rec[r], b_gate_i[r], b_gate_f[r], g_mlstm_norm[r], w_dw[r], b_dw[r],
                  g_conv_norm[r], b_conv_norm[r], w_out_rec[r])
            mp, st_p = rec_mixer(xp, zC, zn, zm, zbuf, *wr)
            ms, st_s = rec_mixer(xs, state_mlstm_C[r], state_mlstm_n[r], state_mlstm_m[r], state_conv[r], *wr)
            rec_p.append(st_p)
            rec_s.append(st_s)
        else:
            a = l // 2
            mp, k_p, v_p = attn_prompt(xp, w_qkv_attn[a], w_o_attn[a], bias_t)
            ms, k_s, v_s = attn_sample(xs, cache_k, cache_v, a, page_table, w_qkv_attn[a], w_o_attn[a], bias_t)
            kv_p.append((k_p, v_p))
            kv_s.append((k_s, v_s))
        xp = layer_norm(DEEPNORM_ALPHA * xp + mp, ln_mix_g[l], ln_mix_b[l])
        xs = layer_norm(DEEPNORM_ALPHA * xs + ms, ln_mix_g[l], ln_mix_b[l])
        xp = layer_norm(DEEPNORM_ALPHA * xp + moe(xp, w_router, b_router, w1_exp[l], w3_exp[l], w2_exp[l]),
                        ln_ffn_g[l], ln_ffn_b[l])
        xs = layer_norm(DEEPNORM_ALPHA * xs + moe(xs, w_router, b_router, w1_exp[l], w3_exp[l], w2_exp[l]),
                        ln_ffn_g[l], ln_ffn_b[l])
    pdt, sdt = x_prompt.dtype, state_mlstm_C.dtype
    mC_p = jnp.stack([s[0] for s in rec_p]).astype(pdt)
    mn_p = jnp.stack([s[1] for s in rec_p]).astype(pdt)
    mm_p = jnp.stack([s[2] for s in rec_p]).astype(pdt)
    conv_p = jnp.stack([s[3] for s in rec_p])
    k_prompt = jnp.stack([t[0] for t in kv_p])
    v_prompt = jnp.stack([t[1] for t in kv_p])
    mC_s = jnp.stack([s[0] for s in rec_s]).astype(sdt)
    mn_s = jnp.stack([s[1] for s in rec_s]).astype(sdt)
    mm_s = jnp.stack([s[2] for s in rec_s]).astype(sdt)
    conv_s = jnp.stack([s[3] for s in rec_s])
    k_sample = jnp.stack([t[0] for t in kv_s])
    v_sample = jnp.stack([t[1] for t in kv_s])
    return (xp, xs, mC_p, mn_p, mm_p, conv_p, k_prompt, v_prompt, mC_s, mn_s, mm_s, conv_s, k_sample, v_sample)
```

```python
import functools
import math

import numpy as np
import jax
import jax.numpy as jnp
from jax import lax
from jax.experimental import pallas as pl
from jax.experimental.pallas import tpu as pltpu

F32 = jnp.float32
BF16 = jnp.bfloat16
I32 = jnp.int32

M_HEADS = 4
M_CHUNK = 64
CONV_WIDTH = 31
A_HEADS = 16
MOBA_BLOCK = 256
MOBA_TOPK = 3
PAGE_SIZE = 128
N_BUCKETS = 32
T5_MAX_DIST = 128
N_EXPERTS = 16
N_GROUPS = 4
EXP_PER_GROUP = N_EXPERTS // N_GROUPS
TOP_K = 2
LN_EPS = 1e-5
NEG_INF = -1e30

LANES = 128
SUBLANES = 8
VMEM_LIMIT_BYTES = 56 * 1024 * 1024

CONV_HALO = 32
ROUTE_TOKENS = 256
MOE_TILE = 512
MOE_CHUNK = 256

NT_DIMS = (((1,), (1,)), ((), ()))
TN_DIMS = (((0,), (0,)), ((), ()))


def _cparams(n_axes):
    return pltpu.CompilerParams(dimension_semantics=("arbitrary",) * n_axes,
                                vmem_limit_bytes=VMEM_LIMIT_BYTES)


def _row_tile(t, cap=1536):
    best = None
    for cand in range(16, cap + 1, 16):
        if t % cand == 0:
            best = cand
    assert best is not None, t
    return best


def _t5_bucket_starts():
    exact = N_BUCKETS // 2
    d = np.arange(0, 4 * T5_MAX_DIST)
    nf = np.maximum(d, 1).astype(np.float64)
    large = exact + (np.log(nf / exact) / math.log(T5_MAX_DIST / exact) * (N_BUCKETS - exact)).astype(np.int64)
    bucket = np.where(d < exact, d, np.minimum(large, N_BUCKETS - 1))
    starts = [int(d[bucket >= b].min()) for b in range(N_BUCKETS)]
    assert starts[-1] <= MOBA_BLOCK, "blocks two or more back must share the last bucket"
    return starts


def _mm_body(x_ref, w_ref, o_ref, wbf_ref):
    @pl.when(pl.program_id(1) == 0)
    def _():
        wbf_ref[...] = w_ref[...].astype(BF16)

    o_ref[...] = jnp.dot(x_ref[...].astype(BF16), wbf_ref[...],
                         preferred_element_type=F32).astype(o_ref.dtype)


def matmul(x, w, *, lead=None, col0=0, n, tn, out_dtype, name):
    m, k = x.shape
    tm = _row_tile(m)
    assert n % tn == 0 and col0 % tn == 0 and w.shape[-2] == k
    jb = col0 // tn
    if lead is None:
        w_spec = pl.BlockSpec((k, tn), lambda j, i: (0, j + jb))
    else:
        w_spec = pl.BlockSpec((None, k, tn), lambda j, i: (lead, 0, j + jb))
    return pl.pallas_call(
        _mm_body,
        grid=(n // tn, m // tm),
        in_specs=[pl.BlockSpec((tm, k), lambda j, i: (i, 0)), w_spec],
        out_specs=pl.BlockSpec((tm, tn), lambda j, i: (i, j)),
        out_shape=jax.ShapeDtypeStruct((m, n), out_dtype),
        scratch_shapes=[pltpu.VMEM((k, tn), BF16)],
        compiler_params=_cparams(2),
        name=name,
    )(x, w)


def _layer_norm_rows(z, g, b):
    mu = jnp.mean(z, axis=-1, keepdims=True)
    zc = z - mu
    var = jnp.mean(zc * zc, axis=-1, keepdims=True)
    return zc * lax.rsqrt(var + LN_EPS) * g + b


def _ln_body(x_ref, f_ref, g_ref, b_ref, o_ref, obf_ref, *, alpha):
    y = _layer_norm_rows(alpha * x_ref[...] + f_ref[...], g_ref[...], b_ref[...])
    o_ref[...] = y
    obf_ref[...] = y.astype(BF16)


def ln_residual(x, f, g, b, alpha, name):
    t, d = x.shape
    tm = _row_tile(t, 768)
    row = pl.BlockSpec((tm, d), lambda i: (i, 0))
    vec = pl.BlockSpec((1, d), lambda i: (0, 0))
    return pl.pallas_call(
        functools.partial(_ln_body, alpha=alpha),
        grid=(t // tm,),
        in_specs=[row, row, vec, vec],
        out_specs=[row, row],
        out_shape=[jax.ShapeDtypeStruct((t, d), F32), jax.ShapeDtypeStruct((t, d), BF16)],
        compiler_params=_cparams(1),
        name=name,
    )(x, f, g.reshape(1, d), b.reshape(1, d))


def _mlstm_body(bg_ref, q_ref, k_ref, v_ref, g_ref, og_ref, gh_ref, c0_ref, n0_ref, m0_ref,
                h_ref, c_ref, n_ref, m_ref, *, chunk, n_chunks, scale):
    L = chunk
    head = pl.program_id(1)
    c_ref[0, 0] = c0_ref[0, 0]
    n_ref[0, 0] = n0_ref[0, 0]
    m_ref[0, 0] = m0_ref[0, 0]
    b_i = bg_ref[head]
    b_f = bg_ref[M_HEADS + head]
    row = lax.broadcasted_iota(I32, (L, L), 0)
    col = lax.broadcasted_iota(I32, (L, L), 1)
    causal = col <= row
    eye = col == row
    lane = lax.broadcasted_iota(I32, (L, LANES), 1)

    def one_chunk(r0):
        rows = pl.ds(r0, L)
        g = g_ref[rows, :]
        li = jnp.sum(jnp.where(lane == head, g, 0.0), axis=1, keepdims=True) + b_i
        fg = jnp.sum(jnp.where(lane == head + M_HEADS, g, 0.0), axis=1, keepdims=True) + b_f
        lf = -(jnp.maximum(-fg, 0.0) + jnp.log1p(jnp.exp(-jnp.abs(fg))))
        lf_b = jnp.broadcast_to(lf, (L, L))
        li_b = jnp.broadcast_to(li, (L, L))
        lf_row = jnp.sum(jnp.where(eye, lf_b, 0.0), axis=0, keepdims=True)
        li_row = jnp.sum(jnp.where(eye, li_b, 0.0), axis=0, keepdims=True)
        b_col = jnp.sum(jnp.where(causal, jnp.broadcast_to(lf_row, (L, L)), 0.0), axis=1, keepdims=True)
        b_row = jnp.sum(jnp.where(row <= col, lf_b, 0.0), axis=0, keepdims=True)
        m_old = m_ref[0, 0][:, :1]
        dm = jnp.where(causal, b_col - b_row + li_row, NEG_INF)
        inter = b_col + m_old
        mt = jnp.maximum(inter, jnp.max(dm, axis=1, keepdims=True))
        w_int = jnp.exp(inter - mt)
        qq = (q_ref[rows, :] * scale).astype(BF16)
        kk = k_ref[rows, :]
        kb = kk.astype(BF16)
        vb = v_ref[rows, :].astype(BF16)
        c_old = c_ref[0, 0]
        n_old = n_ref[0, 0]
        a_ts = jnp.exp(dm - mt) * lax.dot_general(qq, kb, NT_DIMS, preferred_element_type=F32)
        num = (w_int * lax.dot_general(qq, c_old.astype(BF16), NT_DIMS, preferred_element_type=F32)
               + jnp.dot(a_ts.astype(BF16), vb, preferred_element_type=F32))
        qn = jnp.sum(qq.astype(F32) * n_old.astype(BF16).astype(F32), axis=1, keepdims=True)
        den = w_int * qn + jnp.sum(a_ts, axis=1, keepdims=True)
        hh = num / jnp.maximum(jnp.abs(den), jnp.exp(-mt))
        g_tot = b_col[L - 1:L, :]
        dec_col = g_tot - b_col + li
        dec_row = g_tot - b_row + li_row
        m_new = jnp.maximum(g_tot + m_old, jnp.max(dec_row, axis=1, keepdims=True))
        a_old = jnp.exp(g_tot + m_old - m_new)
        wk = jnp.exp(dec_col - m_new)
        c_ref[0, 0] = a_old * c_old + lax.dot_general(vb, (kk * wk).astype(BF16), TN_DIMS,
                                                      preferred_element_type=F32)
        n_ref[0, 0] = a_old * n_old + jnp.sum(kb.astype(F32) * wk.astype(BF16).astype(F32), axis=0, keepdims=True)
        m_ref[0, 0] = jnp.broadcast_to(m_new, (1, LANES))
        mu = jnp.mean(hh, axis=-1, keepdims=True)
        hc = hh - mu
        var = jnp.mean(hc * hc, axis=-1, keepdims=True)
        hn = hc * lax.rsqrt(var + LN_EPS) * gh_ref[...]
        h_ref[rows, :] = (hn * jax.nn.sigmoid(og_ref[rows, :])).astype(h_ref.dtype)

    if n_chunks == 1:
        one_chunk(0)
    else:
        def loop_body(c, carry):
            one_chunk(pl.multiple_of(c * L, L))
            return carry
        lax.fori_loop(0, n_chunks, loop_body, 0)


def mlstm(u1, gates, u2, b_gates, g_h, c0, n0, m0, *, seq, row_block0, out_dtype, name):
    bn, heads, dv, dk = c0.shape
    chunk = math.gcd(seq, M_CHUNK)
    n_chunks = seq // chunk
    kv_blk = (2 * heads * dk) // dv
    assert (2 * heads * dk) % dv == 0
    m0b = jnp.broadcast_to(m0.astype(F32)[:, :, None, None], (bn, heads, 1, LANES))
    body = functools.partial(_mlstm_body, chunk=chunk, n_chunks=n_chunks, scale=dk ** -0.5)
    st4 = lambda b, h: (b, h, 0, 0)
    h, c, n, m = pl.pallas_call(
        body,
        grid=(bn, heads),
        in_specs=[
            pl.BlockSpec(memory_space=pltpu.SMEM),
            pl.BlockSpec((seq, dk), lambda b, h: (row_block0 + b, h)),
            pl.BlockSpec((seq, dk), lambda b, h: (row_block0 + b, heads + h)),
            pl.BlockSpec((seq, dv), lambda b, h: (row_block0 + b, kv_blk + h)),
            pl.BlockSpec((seq, LANES), lambda b, h: (row_block0 + b, 0)),
            pl.BlockSpec((seq, dv), lambda b, h: (row_block0 + b, h)),
            pl.BlockSpec((1, dv), lambda b, h: (0, h)),
            pl.BlockSpec((1, 1, dv, dk), st4),
            pl.BlockSpec((1, 1, 1, dk), st4),
            pl.BlockSpec((1, 1, 1, LANES), st4),
        ],
        out_specs=[
            pl.BlockSpec((seq, dv), lambda b, h: (b, h)),
            pl.BlockSpec((1, 1, dv, dk), st4),
            pl.BlockSpec((1, 1, 1, dk), st4),
            pl.BlockSpec((1, 1, 1, LANES), st4),
        ],
        out_shape=[
            jax.ShapeDtypeStruct((bn * seq, heads * dv), out_dtype),
            jax.ShapeDtypeStruct((bn, heads, dv, dk), F32),
            jax.ShapeDtypeStruct((bn, heads, 1, dk), F32),
            jax.ShapeDtypeStruct((bn, heads, 1, LANES), F32),
        ],
        compiler_params=_cparams(2),
        name=name,
    )(b_gates, u1, u1, u1, gates, u2, g_h.reshape(1, heads * dv),
      c0.astype(F32), n0.astype(F32).reshape(bn, heads, 1, dk), m0b)
    return h, c, n[:, :, 0, :], m[:, :, 0, 0]


def _conv_body(ga_ref, gb_ref, buf0_ref, w_ref, bdw_ref, gcn_ref, bcn_ref, c_ref, nb_ref, win_ref,
               *, ts, n_t, row_group):
    t = pl.program_id(1)
    w1 = CONV_WIDTH - 1
    lead = CONV_HALO - w1

    @pl.when(t == 0)
    def _():
        win_ref[0:lead, :] = jnp.zeros((lead, win_ref.shape[1]), F32)
        win_ref[lead:CONV_HALO, :] = buf0_ref[0]

    if n_t > 1:
        @pl.when(t > 0)
        def _():
            win_ref[0:CONV_HALO, :] = win_ref[ts:ts + CONV_HALO, :]

    win_ref[CONV_HALO:CONV_HALO + ts, :] = ga_ref[...] * jax.nn.sigmoid(gb_ref[...])
    for r0 in range(0, ts, row_group):
        acc = jnp.broadcast_to(bdw_ref[...], (row_group, win_ref.shape[1]))
        for j in range(CONV_WIDTH):
            acc = acc + w_ref[j:j + 1, :] * win_ref[r0 + lead + j:r0 + lead + j + row_group, :]
        y = _layer_norm_rows(acc, gcn_ref[...], bcn_ref[...])
        c_ref[r0:r0 + row_group, :] = (y * jax.nn.sigmoid(y)).astype(c_ref.dtype)

    @pl.when(t == n_t - 1)
    def _():
        nb_ref[0] = win_ref[ts + lead:ts + CONV_HALO, :]


def conv_module(u2, buf0, w_dw, b_dw, g_cn, b_cn, *, seq, ts, row_block0, col_block0, out_dtype, name):
    bn, w1, ch = buf0.shape
    assert w1 == CONV_WIDTH - 1 and seq % ts == 0
    n_t = seq // ts
    row_group = min(ts, 32)
    body = functools.partial(_conv_body, ts=ts, n_t=n_t, row_group=row_group)
    vec = pl.BlockSpec((1, ch), lambda b, t: (0, 0))
    c, nb = pl.pallas_call(
        body,
        grid=(bn, n_t),
        in_specs=[
            pl.BlockSpec((ts, ch), lambda b, t: (row_block0 + b * n_t + t, col_block0)),
            pl.BlockSpec((ts, ch), lambda b, t: (row_block0 + b * n_t + t, col_block0 + 1)),
            pl.BlockSpec((1, w1, ch), lambda b, t: (b, 0, 0)),
            pl.BlockSpec((CONV_WIDTH, ch), lambda b, t: (0, 0)),
            vec, vec, vec,
        ],
        out_specs=[
            pl.BlockSpec((ts, ch), lambda b, t: (b * n_t + t, 0)),
            pl.BlockSpec((1, w1, ch), lambda b, t: (b, 0, 0)),
        ],
        out_shape=[
            jax.ShapeDtypeStruct((bn * seq, ch), out_dtype),
            jax.ShapeDtypeStruct((bn, w1, ch), F32),
        ],
        scratch_shapes=[pltpu.VMEM((CONV_HALO + ts, ch), F32)],
        compiler_params=_cparams(2),
        name=name,
    )(u2, u2, buf0, w_dw, b_dw.reshape(1, ch), g_cn.reshape(1, ch), b_cn.reshape(1, ch))
    return c, nb


def _bias_body(tbl_ref, o_ref, *, starts):
    head = pl.program_id(0)
    blk = o_ref.shape[-1]
    r = lax.broadcasted_iota(I32, (blk, blk), 0)
    c = lax.broadcasted_iota(I32, (blk, blk), 1)
    for which in range(2):
        d = jnp.maximum(r - c + which * blk, 0)
        val = jnp.full((blk, blk), tbl_ref[0, head], F32)
        for b in range(1, N_BUCKETS):
            val = jnp.where(d >= starts[b], tbl_ref[b, head], val)
        o_ref[0, which] = val


def bias_tiles(rel_bias):
    heads = rel_bias.shape[1]
    return pl.pallas_call(
        functools.partial(_bias_body, starts=_t5_bucket_starts()),
        grid=(heads,),
        in_specs=[pl.BlockSpec(memory_space=pltpu.SMEM)],
        out_specs=pl.BlockSpec((1, 2, MOBA_BLOCK, MOBA_BLOCK), lambda h: (h, 0, 0, 0)),
        out_shape=jax.ShapeDtypeStruct((heads, 2, MOBA_BLOCK, MOBA_BLOCK), F32),
        compiler_params=_cparams(1),
        name="t5_bias_tiles",
    )(rel_bias.astype(F32))


def _moba_body(far_ref, q_ref, k_ref, v_ref, bias_ref, o_ref, ko_ref, vo_ref, means_ref, *, nb, scale):
    blk = MOBA_BLOCK
    head = pl.program_id(1)
    i = pl.program_id(2)

    @pl.when(i == 0)
    def _():
        for j in range(nb):
            means_ref[j:j + 1, :] = jnp.mean(k_ref[j * blk:(j + 1) * blk, :], axis=0, keepdims=True)
        ko_ref[0, 0] = k_ref[...]
        vo_ref[0, 0] = v_ref[...]

    q = q_ref[...]
    qb = q.astype(BF16)
    sc = lax.dot_general(qb, means_ref[...].astype(BF16), NT_DIMS, preferred_element_type=F32)
    jio = lax.broadcasted_iota(I32, (blk, nb), 1)
    rank = jnp.zeros((blk, nb), F32)
    for n in range(nb):
        sn = sc[:, n:n + 1]
        beats = (sn > sc) | ((sn == sc) & (n < jio))
        rank = rank + jnp.where(beats, jnp.where(n < i, 1.0, 0.0), 0.0)
    selmat = jnp.where((rank < MOBA_TOPK) & (jio < i), 1.0, 0.0)

    row = lax.broadcasted_iota(I32, (blk, blk), 0)
    col = lax.broadcasted_iota(I32, (blk, blk), 1)
    far = far_ref[head]

    def block_logits(j0):
        kb = k_ref[pl.ds(j0, blk), :].astype(BF16)
        return lax.dot_general(qb, kb, NT_DIMS, preferred_element_type=F32) * scale

    i0 = pl.multiple_of(i * blk, blk)
    lg = jnp.where(col <= row, block_logits(i0) + bias_ref[0, 0], NEG_INF)
    m0 = jnp.max(lg, axis=1, keepdims=True)
    p = jnp.exp(lg - m0)
    l0 = jnp.sum(p, axis=1, keepdims=True)
    acc0 = jnp.dot(p.astype(BF16), v_ref[pl.ds(i0, blk), :].astype(BF16), preferred_element_type=F32)

    def past_block(j, carry):
        m, l, acc = carry
        j0 = pl.multiple_of(j * blk, blk)
        bias = jnp.where(jnp.full((blk, blk), j, I32) == i - 1, bias_ref[0, 1], far)
        sel_j = jnp.sum(jnp.where(jio == j, selmat, 0.0), axis=1, keepdims=True)
        lg = jnp.where(sel_j > 0.0, block_logits(j0) + bias, NEG_INF)
        m_new = jnp.maximum(m, jnp.max(lg, axis=1, keepdims=True))
        a = jnp.exp(m - m_new)
        p = jnp.exp(lg - m_new)
        l_new = a * l + jnp.sum(p, axis=1, keepdims=True)
        acc_new = a * acc + jnp.dot(p.astype(BF16), v_ref[pl.ds(j0, blk), :].astype(BF16),
                                    preferred_element_type=F32)
        return m_new, l_new, acc_new

    m, l, acc = lax.fori_loop(0, i, past_block, (m0, l0, acc0))
    o_ref[...] = (acc / l).astype(o_ref.dtype)


def moba_prompt(qkv, bias, far, *, batch, seq, name):
    heads = A_HEADS
    dh = qkv.shape[1] // (3 * heads)
    assert seq % MOBA_BLOCK == 0
    nb = seq // MOBA_BLOCK
    body = functools.partial(_moba_body, nb=nb, scale=dh ** -0.5)
    kv_spec = pl.BlockSpec((1, 1, seq, dh), lambda b, h, i: (b, h, 0, 0))
    return pl.pallas_call(
        body,
        grid=(batch, heads, nb),
        in_specs=[
            pl.BlockSpec(memory_space=pltpu.SMEM),
            pl.BlockSpec((MOBA_BLOCK, dh), lambda b, h, i: (b * nb + i, h)),
            pl.BlockSpec((seq, dh), lambda b, h, i: (b, heads + h)),
            pl.BlockSpec((seq, dh), lambda b, h, i: (b, 2 * heads + h)),
            pl.BlockSpec((1, 2, MOBA_BLOCK, MOBA_BLOCK), lambda b, h, i: (h, 0, 0, 0)),
        ],
        out_specs=[
            pl.BlockSpec((MOBA_BLOCK, dh), lambda b, h, i: (b * nb + i, h)),
            kv_spec, kv_spec,
        ],
        out_shape=[
            jax.ShapeDtypeStruct((batch * seq, heads * dh), BF16),
            jax.ShapeDtypeStruct((batch, heads, seq, dh), F32),
            jax.ShapeDtypeStruct((batch, heads, seq, dh), F32),
        ],
        scratch_shapes=[pltpu.VMEM((nb, dh), F32)],
        compiler_params=_cparams(3),
        name=name,
    )(far, qkv, qkv, qkv, bias)


def _page_sum_body(pt_ref, ck_ref, o_ref):
    del pt_ref
    p = pl.program_id(2)
    n_layers, heads = ck_ref.shape[1], ck_ref.shape[2]

    @pl.when(p == 0)
    def _():
        for a in range(n_layers):
            for h in range(heads):
                o_ref[0, 0, a, h:h + 1, :] = jnp.sum(ck_ref[0, a, h], axis=0, keepdims=True)

    @pl.when(p > 0)
    def _():
        for a in range(n_layers):
            for h in range(heads):
                o_ref[0, 0, a, h:h + 1, :] += jnp.sum(ck_ref[0, a, h], axis=0, keepdims=True)


def block_key_sums(cache_k, page_table, n_full, ppb):
    _, n_layers, heads, page, dh = cache_k.shape
    db = page_table.shape[0]
    grid_spec = pltpu.PrefetchScalarGridSpec(
        num_scalar_prefetch=1,
        grid=(db, n_full, ppb),
        in_specs=[pl.BlockSpec((1, n_layers, heads, page, dh),
                               lambda b, n, p, pt: (pt[b, n * ppb + p], 0, 0, 0, 0))],
        out_specs=pl.BlockSpec((1, 1, n_layers, heads, dh), lambda b, n, p, pt: (b, n, 0, 0, 0)),
    )
    return pl.pallas_call(
        _page_sum_body,
        grid_spec=grid_spec,
        out_shape=jax.ShapeDtypeStruct((db, n_full, n_layers, heads, dh), F32),
        compiler_params=_cparams(3),
        name="block_key_sums",
    )(page_table, cache_k)


def _sample_select_body(q_ref, ksum_ref, o_ref):
    heads, sn, _ = q_ref.shape[1:]
    n_full = ksum_ref.shape[2]
    lane = lax.broadcasted_iota(I32, (sn, n_full), 1).astype(F32)
    out_lane = lax.broadcasted_iota(I32, (sn, LANES), 1)
    for h in range(heads):
        means = (ksum_ref[0, h] * (1.0 / MOBA_BLOCK)).astype(BF16)
        sc = lax.dot_general(q_ref[0, h].astype(BF16), means, NT_DIMS, preferred_element_type=F32)
        out = jnp.zeros((sn, LANES), I32)
        for s in range(min(MOBA_TOPK, n_full)):
            mx = jnp.max(sc, axis=1, keepdims=True)
            idx = jnp.min(jnp.where(sc == mx, lane, float(n_full)), axis=1, keepdims=True)
            out = jnp.where(out_lane == s, idx.astype(I32), out)
            sc = jnp.where(lane == idx, -jnp.inf, sc)
        o_ref[0, h] = out


def sample_select(q, ksum):
    db, heads, sn, dh = q.shape
    n_full = ksum.shape[2]
    return pl.pallas_call(
        _sample_select_body,
        grid=(db,),
        in_specs=[pl.BlockSpec((1, heads, sn, dh), lambda b: (b, 0, 0, 0)),
                  pl.BlockSpec((1, heads, n_full, dh), lambda b: (b, 0, 0, 0))],
        out_specs=pl.BlockSpec((1, heads, sn, LANES), lambda b: (b, 0, 0, 0)),
        out_shape=jax.ShapeDtypeStruct((db, heads, sn, LANES), I32),
        compiler_params=_cparams(1),
        name="moba_sample_select",
    )(q, ksum)


def _sample_attend_body(pages_ref, blks_ref, far_ref, q_ref, kn_ref, vn_ref, bias_ref, ck_hbm, cv_hbm,
                        o_ref, kbuf, vbuf, sem, *, layer, n_full, ppb, scale):
    b = pl.program_id(0)
    head = pl.program_id(1)
    heads = pl.num_programs(1)
    sn = q_ref.shape[2]
    n_slots = kbuf.shape[1]
    ksel = n_slots // ppb
    base = (b * heads + head) * sn

    def copies(t, s):
        pg = pages_ref[(base + t) * n_slots + s]
        return (pltpu.make_async_copy(ck_hbm.at[pg, layer, head], kbuf.at[t, s], sem.at[0]),
                pltpu.make_async_copy(cv_hbm.at[pg, layer, head], vbuf.at[t, s], sem.at[1]))

    for t in range(sn):
        for s in range(n_slots):
            ck, cv = copies(t, s)
            ck.start()
            cv.start()
    for t in range(sn):
        for s in range(n_slots):
            ck, cv = copies(t, s)
            ck.wait()
            cv.wait()

    far = far_ref[head]
    width = ppb * PAGE_SIZE
    kn = kn_ref[0, 0].astype(BF16)
    vn = vn_ref[0, 0].astype(BF16)
    own_col = lax.broadcasted_iota(I32, (1, sn), 1)
    for t in range(sn):
        qt = q_ref[0, 0, t:t + 1, :].astype(BF16)
        lg_own = lax.dot_general(qt, kn, NT_DIMS, preferred_element_type=F32) * scale + bias_ref[0, 0, t:t + 1, :sn]
        lg_own = jnp.where(own_col <= t, lg_own, NEG_INF)
        logits = []
        for s in range(ksel):
            kt = kbuf[t, s * ppb:(s + 1) * ppb].reshape(width, kbuf.shape[-1]).astype(BF16)
            blk = blks_ref[(base + t) * ksel + s]
            newest = jnp.full((1, width), blk, I32) == n_full - 1
            bias = jnp.where(newest, bias_ref[0, 1, t:t + 1, :], far)
            logits.append(lax.dot_general(qt, kt, NT_DIMS, preferred_element_type=F32) * scale + bias)
        mx = jnp.max(lg_own, axis=1, keepdims=True)
        for lg in logits:
            mx = jnp.maximum(mx, jnp.max(lg, axis=1, keepdims=True))
        p_own = jnp.exp(lg_own - mx)
        den = jnp.sum(p_own, axis=1, keepdims=True)
        acc = jnp.dot(p_own.astype(BF16), vn, preferred_element_type=F32)
        for s, lg in enumerate(logits):
            p = jnp.exp(lg - mx)
            den = den + jnp.sum(p, axis=1, keepdims=True)
            vt = vbuf[t, s * ppb:(s + 1) * ppb].reshape(width, vbuf.shape[-1]).astype(BF16)
            acc = acc + jnp.dot(p.astype(BF16), vt, preferred_element_type=F32)
        o_ref[0, 0, t:t + 1, :] = acc / den


def sample_attend(q, k_new, v_new, pages, blks, bias, far, cache_k, cache_v, *, layer, n_full, ppb, name):
    db, heads, sn, dh = q.shape
    n_slots = pages.shape[0] // (db * heads * sn)
    body = functools.partial(_sample_attend_body, layer=layer, n_full=n_full, ppb=ppb, scale=dh ** -0.5)
    tok = pl.BlockSpec((1, 1, sn, dh), lambda b, h, *_: (b, h, 0, 0))
    grid_spec = pltpu.PrefetchScalarGridSpec(
        num_scalar_prefetch=2,
        grid=(db, heads),
        in_specs=[
            pl.BlockSpec(memory_space=pltpu.SMEM),
            tok, tok, tok,
            pl.BlockSpec((1, 2, sn, MOBA_BLOCK), lambda b, h, *_: (h, 0, 0, 0)),
            pl.BlockSpec(memory_space=pl.ANY),
            pl.BlockSpec(memory_space=pl.ANY),
        ],
        out_specs=tok,
        scratch_shapes=[
            pltpu.VMEM((sn, n_slots, PAGE_SIZE, dh), F32),
            pltpu.VMEM((sn, n_slots, PAGE_SIZE, dh), F32),
            pltpu.SemaphoreType.DMA((2,)),
        ],
    )
    return pl.pallas_call(
        body,
        grid_spec=grid_spec,
        out_shape=jax.ShapeDtypeStruct((db, heads, sn, dh), F32),
        compiler_params=_cparams(2),
        name=name,
    )(pages, blks, far, q, k_new, v_new, bias, cache_k, cache_v)


def _top2_of(vals):
    n = len(vals)
    m1 = functools.reduce(jnp.maximum, vals)
    i1 = jnp.full(m1.shape, n - 1, I32)
    for k in range(n - 2, -1, -1):
        i1 = jnp.where(vals[k] == m1, k, i1)
    rest = [jnp.where(i1 == k, -1.0, vals[k]) for k in range(n)]
    m2 = functools.reduce(jnp.maximum, rest)
    i2 = jnp.full(m2.shape, n - 1, I32)
    for k in range(n - 2, -1, -1):
        i2 = jnp.where(rest[k] == m2, k, i2)
    return m1, i1, m2, i2


def _route_body(x_ref, wrt_ref, br_ref, e_ref, g_ref, r_ref, cnt_ref, carry_ref, *, n_tokens):
    step = pl.program_id(0)
    tr = x_ref.shape[0]

    @pl.when(step == 0)
    def _():
        carry_ref[...] = jnp.zeros(carry_ref.shape, F32)

    tok = step * tr + lax.broadcasted_iota(I32, (1, tr), 1)
    valid = tok < n_tokens
    logits = lax.dot_general(wrt_ref[...].astype(BF16), x_ref[...].astype(BF16), NT_DIMS,
                             preferred_element_type=F32)
    logits = jnp.where(valid, logits + br_ref[...], 0.0)
    ex = jnp.exp(logits - jnp.max(logits, axis=0, keepdims=True))
    aff = ex / jnp.sum(ex, axis=0, keepdims=True)
    a = [aff[e:e + 1, :] for e in range(N_EXPERTS)]
    group_best = []
    for g in range(N_GROUPS):
        m1, _, m2, _ = _top2_of(a[g * EXP_PER_GROUP:(g + 1) * EXP_PER_GROUP])
        group_best.append(m1 + m2)
    gmax = functools.reduce(jnp.maximum, group_best)
    g_sel = jnp.full(gmax.shape, N_GROUPS - 1, I32)
    for g in range(N_GROUPS - 2, -1, -1):
        g_sel = jnp.where(group_best[g] == gmax, g, g_sel)
    in_g = []
    for k in range(EXP_PER_GROUP):
        v = a[(N_GROUPS - 1) * EXP_PER_GROUP + k]
        for g in range(N_GROUPS - 2, -1, -1):
            v = jnp.where(g_sel == g, a[g * EXP_PER_GROUP + k], v)
        in_g.append(v)
    v1, i1, v2, i2 = _top2_of(in_g)
    e0 = g_sel * EXP_PER_GROUP + i1
    e1 = g_sel * EXP_PER_GROUP + i2
    tot = v1 + v2
    eio = lax.broadcasted_iota(I32, (N_EXPERTS, tr), 0)
    hit0 = (eio == e0) & valid
    hit1 = (eio == e1) & valid
    cnt = jnp.where(hit0 | hit1, 1.0, 0.0)
    s_io = lax.broadcasted_iota(I32, (tr, tr), 0)
    t_io = lax.broadcasted_iota(I32, (tr, tr), 1)
    before = jnp.where(s_io < t_io, 1.0, 0.0).astype(BF16)
    prior = jnp.dot(cnt.astype(BF16), before, preferred_element_type=F32) + carry_ref[:, :1]
    rank0 = jnp.sum(jnp.where(hit0, prior, 0.0), axis=0, keepdims=True)
    rank1 = jnp.sum(jnp.where(hit1, prior, 0.0), axis=0, keepdims=True)
    carry_ref[...] = carry_ref[...] + jnp.sum(cnt, axis=1, keepdims=True)
    sub = lax.broadcasted_iota(I32, (SUBLANES, tr), 0)

    def rows01(r0, r1):
        return jnp.where(sub == 0, r0, jnp.where(sub == 1, r1, jnp.zeros_like(r0)))

    e_ref[...] = rows01(e0, e1)
    r_ref[...] = rows01(rank0.astype(I32), rank1.astype(I32))
    g_ref[...] = rows01(v1 / tot, v2 / tot)
    cnt_ref[...] = carry_ref[...].astype(I32)


def moe_route(x, w_router, b_router, name):
    t, d = x.shape
    tr = ROUTE_TOKENS
    n_steps = pl.cdiv(t, tr)
    tpad = n_steps * tr
    blk = pl.BlockSpec((SUBLANES, tr), lambda i: (0, i))
    return pl.pallas_call(
        functools.partial(_route_body, n_tokens=t),
        grid=(n_steps,),
        in_specs=[pl.BlockSpec((tr, d), lambda i: (i, 0)),
                  pl.BlockSpec((N_EXPERTS, d), lambda i: (0, 0)),
                  pl.BlockSpec((N_EXPERTS, 1), lambda i: (0, 0))],
        out_specs=[blk, blk, blk, pl.BlockSpec((N_EXPERTS, LANES), lambda i: (0, 0))],
        out_shape=[jax.ShapeDtypeStruct((SUBLANES, tpad), I32),
                   jax.ShapeDtypeStruct((SUBLANES, tpad), F32),
                   jax.ShapeDtypeStruct((SUBLANES, tpad), I32),
                   jax.ShapeDtypeStruct((N_EXPERTS, LANES), I32)],
        scratch_shapes=[pltpu.VMEM((N_EXPERTS, LANES), F32)],
        compiler_params=_cparams(1),
        name=name,
    )(x, w_router.T.astype(F32), b_router.astype(F32).reshape(N_EXPERTS, 1))


def _dispatch_body(dest_ref, x_ref, xbuf_in, xbuf, sem, *, tpad):
    del xbuf_in
    tb = x_ref.shape[0]
    t0 = pl.program_id(0) * tb

    def copy(r, k):
        d = dest_ref[k * tpad + t0 + r]
        return pltpu.make_async_copy(x_ref.at[pl.ds(r, 1), :], xbuf.at[pl.ds(d, 1), :], sem)

    def start(r, carry):
        for k in range(TOP_K):
            copy(r, k).start()
        return carry

    def wait(r, carry):
        for k in range(TOP_K):
            copy(r, k).wait()
        return carry

    lax.fori_loop(0, tb, start, 0)
    lax.fori_loop(0, tb, wait, 0)


def moe_dispatch(x, dest, n_rows, tpad, name):
    t, d = x.shape
    tb = _row_tile(t)
    grid_spec = pltpu.PrefetchScalarGridSpec(
        num_scalar_prefetch=1,
        grid=(t // tb,),
        in_specs=[pl.BlockSpec((tb, d), lambda i, dest: (i, 0)),
                  pl.BlockSpec(memory_space=pl.ANY)],
        out_specs=pl.BlockSpec(memory_space=pl.ANY),
        scratch_shapes=[pltpu.SemaphoreType.DMA(())],
    )
    return pl.pallas_call(
        functools.partial(_dispatch_body, tpad=tpad),
        grid_spec=grid_spec,
        out_shape=jax.ShapeDtypeStruct((n_rows, d), x.dtype),
        input_output_aliases={2: 0},
        compiler_params=_cparams(1),
        name=name,
    )(dest, x, jnp.zeros((n_rows, d), x.dtype))


def _experts_body(te_ref, nu_ref, x_ref, w1_ref, w3_ref, w2_ref, o_ref):
    del te_ref
    i = pl.program_id(0)
    c = pl.program_id(1)

    @pl.when(i < nu_ref[0])
    def _():
        xb = x_ref[...].astype(BF16)
        h1 = jnp.dot(xb, w1_ref[...].astype(BF16), preferred_element_type=F32)
        h3 = jnp.dot(xb, w3_ref[...].astype(BF16), preferred_element_type=F32)
        hb = (h1 * jax.nn.sigmoid(h1) * h3).astype(BF16)
        y = jnp.dot(hb, w2_ref[...].astype(BF16), preferred_element_type=F32)

        @pl.when(c == 0)
        def _():
            o_ref[...] = y

        @pl.when(c > 0)
        def _():
            o_ref[...] += y

    @pl.when((i >= nu_ref[0]) & (c == 0))
    def _():
        o_ref[...] = jnp.zeros(o_ref.shape, o_ref.dtype)


def moe_experts(xbuf, tile_expert, n_used, w1, w3, w2, layer, name):
    n_rows, d = xbuf.shape
    de = w1.shape[-1]
    tm, tc = MOE_TILE, MOE_CHUNK
    n_tiles, n_chunks = n_rows // tm, de // tc

    def tile(i, nu):
        return jnp.minimum(i, nu[0] - 1)

    def chunk(i, c, nu):
        return jnp.where(i < nu[0], c, n_chunks - 1)

    grid_spec = pltpu.PrefetchScalarGridSpec(
        num_scalar_prefetch=2,
        grid=(n_tiles, n_chunks),
        in_specs=[
            pl.BlockSpec((tm, d), lambda i, c, te, nu: (tile(i, nu), 0)),
            pl.BlockSpec((None, None, d, tc), lambda i, c, te, nu: (layer, te[tile(i, nu)], 0, chunk(i, c, nu))),
            pl.BlockSpec((None, None, d, tc), lambda i, c, te, nu: (layer, te[tile(i, nu)], 0, chunk(i, c, nu))),
            pl.BlockSpec((None, None, tc, d), lambda i, c, te, nu: (layer, te[tile(i, nu)], chunk(i, c, nu), 0)),
        ],
        out_specs=pl.BlockSpec((tm, d), lambda i, c, te, nu: (i, 0)),
    )
    return pl.pallas_call(
        _experts_body,
        grid_spec=grid_spec,
        out_shape=jax.ShapeDtypeStruct((n_rows, d), F32),
        compiler_params=_cparams(2),
        name=name,
    )(tile_expert, n_used, xbuf, w1, w3, w2)


def _combine_body(dest_ref, x_ref, gate_ref, g_ref, b_ref, ybuf, o_ref, obf_ref, y0_ref, y1_ref, sem,
                  *, tpad, alpha):
    tb = x_ref.shape[0]
    t0 = pl.program_id(0) * tb
    bufs = (y0_ref, y1_ref)

    def copy(r, k):
        d = dest_ref[k * tpad + t0 + r]
        return pltpu.make_async_copy(ybuf.at[pl.ds(d, 1), :], bufs[k].at[pl.ds(r, 1), :], sem)

    def start(r, carry):
        for k in range(TOP_K):
            copy(r, k).start()
        return carry

    def wait(r, carry):
        for k in range(TOP_K):
            copy(r, k).wait()
        return carry

    lax.fori_loop(0, tb, start, 0)
    lax.fori_loop(0, tb, wait, 0)
    gate = gate_ref[...]
    moe = gate[:, 0:1] * y0_ref[...] + gate[:, 1:2] * y1_ref[...]
    y = _layer_norm_rows(alpha * x_ref[...] + moe, g_ref[...], b_ref[...])
    o_ref[...] = y
    obf_ref[...] = y.astype(BF16)


def moe_combine_ln(x, ybuf, dest, gate_cols, g, b, alpha, tpad, name):
    t, d = x.shape
    tb = _row_tile(t, 768)
    row = pl.BlockSpec((tb, d), lambda i, dest: (i, 0))
    vec = pl.BlockSpec((1, d), lambda i, dest: (0, 0))
    grid_spec = pltpu.PrefetchScalarGridSpec(
        num_scalar_prefetch=1,
        grid=(t // tb,),
        in_specs=[row, pl.BlockSpec((tb, TOP_K), lambda i, dest: (i, 0)), vec, vec,
                  pl.BlockSpec(memory_space=pl.ANY)],
        out_specs=[row, row],
        scratch_shapes=[pltpu.VMEM((tb, d), F32), pltpu.VMEM((tb, d), F32), pltpu.SemaphoreType.DMA(())],
    )
    return pl.pallas_call(
        functools.partial(_combine_body, tpad=tpad, alpha=alpha),
        grid_spec=grid_spec,
        out_shape=[jax.ShapeDtypeStruct((t, d), F32), jax.ShapeDtypeStruct((t, d), BF16)],
        compiler_params=_cparams(1),
        name=name,
    )(dest, x, gate_cols, g.reshape(1, d), b.reshape(1, d), ybuf)


def moe_layer(x, w_router, b_router, w1, w3, w2, layer, g, b, alpha):
    t, d = x.shape
    tm = MOE_TILE
    e_idx, gate, rank, counts = moe_route(x, w_router, b_router, name=f"moe_route_{layer}")
    tpad = e_idx.shape[1]
    counts = counts[:, 0]
    padded = (counts + tm - 1) // tm * tm
    pad_end = jnp.cumsum(padded)
    pad_start = pad_end - padded
    dest = (pad_start[e_idx[:TOP_K]] + rank[:TOP_K]).reshape(-1).astype(I32)
    n_tiles = (t * TOP_K + N_EXPERTS * (tm - 1)) // tm + 1
    n_used = (pad_end[-1] // tm).astype(I32).reshape(1)
    tile_expert = jnp.minimum(jnp.searchsorted(pad_end, jnp.arange(n_tiles, dtype=I32) * tm, side="right"),
                              N_EXPERTS - 1).astype(I32)
    xbuf = moe_dispatch(x, dest, n_tiles * tm, tpad, name=f"moe_dispatch_{layer}")
    ybuf = moe_experts(xbuf, tile_expert, n_used, w1, w3, w2, layer, name=f"moe_experts_{layer}")
    gate_cols = gate[:TOP_K, :t].T
    return moe_combine_ln(x, ybuf, dest, gate_cols, g, b, alpha, tpad, name=f"moe_combine_{layer}")


def kernel(x_prompt, x_sample, state_mlstm_C, state_mlstm_n, state_mlstm_m, state_conv, cache_k, cache_v, page_table, w_in_rec, b_gate_i, b_gate_f, g_mlstm_norm, w_dw, b_dw, g_conv_norm, b_conv_norm, w_out_rec, w_qkv_attn, w_o_attn, rel_bias, ln_mix_g, ln_mix_b, ln_ffn_g, ln_ffn_b, w_router, b_router, w1_exp, w3_exp, w2_exp):
    bp, seq, d = x_prompt.shape
    db, sn, _ = x_sample.shape
    depth = ln_mix_g.shape[0]
    alpha = (2 * depth) ** 0.25
    tp, ts_ = bp * seq, db * sn
    heads_m, dv, dk = state_mlstm_C.shape[2:]
    ch = state_conv.shape[-1]
    qk_w = 2 * heads_m * dk + heads_m * dv
    n_gate = 2 * heads_m
    rest_w = heads_m * dv + 2 * ch
    assert w_in_rec.shape[-1] == qk_w + n_gate + rest_w and heads_m == M_HEADS
    assert tp % sn == 0 and tp % seq == 0
    dh = d // A_HEADS
    past = page_table.shape[1] * PAGE_SIZE
    ppb = MOBA_BLOCK // PAGE_SIZE
    n_full = past // MOBA_BLOCK
    assert past % MOBA_BLOCK == 0 and n_full >= 1 and sn <= MOBA_BLOCK, "sample keys of the current block must all be new"

    x = jnp.concatenate([x_prompt.reshape(tp, d), x_sample.reshape(ts_, d)], axis=0)
    x_in = x
    sdt = state_mlstm_C.dtype

    bias = bias_tiles(rel_bias)
    far = rel_bias[N_BUCKETS - 1].astype(F32)
    ksums = block_key_sums(cache_k, page_table, n_full, ppb)
    ksums = jnp.transpose(ksums, (2, 0, 3, 1, 4))

    w_rest = w_in_rec[:, :, qk_w + n_gate:]
    w_gate = jnp.pad(w_in_rec[:, :, qk_w:qk_w + n_gate], ((0, 0), (0, 0), (0, LANES - n_gate)))
    z_c = jnp.zeros((bp, heads_m, dv, dk), F32)
    z_n = jnp.zeros((bp, heads_m, dk), F32)
    z_m = jnp.zeros((bp, heads_m), F32)
    z_buf = jnp.zeros((bp, CONV_WIDTH - 1, ch), F32)

    rec_p, rec_s, kv_p, kv_s = [], [], [], []
    for l in range(depth):
        if l % 2 == 0:
            r = l // 2
            u1 = matmul(x_in, w_in_rec, lead=r, n=qk_w, tn=512, out_dtype=F32, name=f"rec_in_qkv_{r}")
            u2 = matmul(x_in, w_rest, lead=r, n=rest_w, tn=512, out_dtype=F32, name=f"rec_in_rest_{r}")
            gates = matmul(x_in, w_gate, lead=r, n=LANES, tn=LANES, out_dtype=F32, name=f"rec_in_gates_{r}")
            b_g = jnp.concatenate([b_gate_i[r], b_gate_f[r]]).astype(F32)
            h_p, c_p, n_p, m_p = mlstm(u1, gates, u2, b_g, g_mlstm_norm[r], z_c, z_n, z_m,
                                       seq=seq, row_block0=0, out_dtype=BF16, name=f"mlstm_prompt_{r}")
            h_s, c_s, n_s, m_s = mlstm(u1, gates, u2, b_g, g_mlstm_norm[r], state_mlstm_C[r], state_mlstm_n[r],
                                       state_mlstm_m[r], seq=sn, row_block0=tp // sn, out_dtype=F32,
                                       name=f"mlstm_sample_{r}")
            conv_cols = (heads_m * dv) // ch
            assert (heads_m * dv) % ch == 0
            cv_p, buf_p = conv_module(u2, z_buf, w_dw[r], b_dw[r], g_conv_norm[r], b_conv_norm[r], seq=seq, ts=128,
                                      row_block0=0, col_block0=conv_cols, out_dtype=BF16, name=f"conv_prompt_{r}")
            cv_s, buf_s = conv_module(u2, state_conv[r].astype(F32), w_dw[r], b_dw[r], g_conv_norm[r], b_conv_norm[r],
                                      seq=sn, ts=sn, row_block0=tp // sn, col_block0=conv_cols, out_dtype=F32,
                                      name=f"conv_sample_{r}")
            mix = jnp.concatenate([jnp.concatenate([h_p, cv_p], axis=1),
                                   jnp.concatenate([h_s, cv_s], axis=1).astype(BF16)], axis=0)
            f = matmul(mix, w_out_rec, lead=r, n=d, tn=512, out_dtype=F32, name=f"rec_out_{r}")
            rec_p.append((c_p, n_p, m_p, buf_p))
            rec_s.append((c_s, n_s, m_s, buf_s))
        else:
            a = l // 2
            qkv = matmul(x_in, w_qkv_attn, lead=a, n=3 * d, tn=512, out_dtype=F32, name=f"attn_qkv_{a}")
            o_p, k_p, v_p = moba_prompt(qkv, bias, far, batch=bp, seq=seq, name=f"moba_prompt_{a}")
            qkv_s = qkv[tp:].reshape(db, sn, 3, A_HEADS, dh).transpose(2, 0, 3, 1, 4)
            q_s, k_s, v_s = qkv_s[0], qkv_s[1], qkv_s[2]
            sel = sample_select(q_s, ksums[a])[..., :MOBA_TOPK]
            page_ids = sel[..., None] * ppb + jnp.arange(ppb, dtype=I32)
            pages = jnp.take_along_axis(page_table[:, None, None, :],
                                        page_ids.reshape(db, A_HEADS, sn, MOBA_TOPK * ppb), axis=-1)
            o_s = sample_attend(q_s, k_s, v_s, pages.reshape(-1).astype(I32), sel.reshape(-1), bias, far,
                                cache_k, cache_v, layer=a, n_full=n_full, ppb=ppb, name=f"moba_sample_{a}")
            o_s = o_s.transpose(0, 2, 1, 3).reshape(ts_, d).astype(BF16)
            f = matmul(jnp.concatenate([o_p, o_s], axis=0), w_o_attn, lead=a, n=d, tn=512, out_dtype=F32,
                       name=f"attn_out_{a}")
            kv_p.append((k_p, v_p))
            kv_s.append((k_s, v_s))
        x, _ = ln_residual(x, f, ln_mix_g[l], ln_mix_b[l], alpha, name=f"ln_mix_{l}")
        x, x_in = moe_layer(x, w_router, b_router, w1_exp, w3_exp, w2_exp, l, ln_ffn_g[l], ln_ffn_b[l], alpha)

    pdt = x_prompt.dtype
    y_p = x[:tp].reshape(bp, seq, d)
    y_s = x[tp:].reshape(db, sn, d)
    return (y_p, y_s,
            jnp.stack([s[0] for s in rec_p]).astype(pdt), jnp.stack([s[1] for s in rec_p]).astype(pdt),
            jnp.stack([s[2] for s in rec_p]).astype(pdt), jnp.stack([s[3] for s in rec_p]),
            jnp.stack([t[0] for t in kv_p]), jnp.stack([t[1] for t in kv_p]),
            jnp.stack([s[0] for s in rec_s]).astype(sdt), jnp.stack([s[1] for s in rec_s]).astype(sdt),
            jnp.stack([s[2] for s in rec_s]).astype(sdt), jnp.stack([s[3] for s in rec_s]),
            jnp.stack([t[0] for t in kv_s]), jnp.stack([t[1] for t in kv_s]))
```

```python
import functools
import math

import numpy as np
import jax
import jax.numpy as jnp
from jax import lax
from jax.experimental import pallas as pl
from jax.experimental.pallas import tpu as pltpu

F32 = jnp.float32
BF16 = jnp.bfloat16
I32 = jnp.int32

M_HEADS = 4
M_CHUNK = 64
CONV_WIDTH = 31
A_HEADS = 16
MOBA_BLOCK = 256
MOBA_TOPK = 3
PAGE_SIZE = 128
N_BUCKETS = 32
T5_MAX_DIST = 128
N_EXPERTS = 16
N_GROUPS = 4
EXP_PER_GROUP = N_EXPERTS // N_GROUPS
TOP_K = 2
LN_EPS = 1e-5
NEG_INF = -1e30

LANES = 128
SUBLANES = 8
VMEM_LIMIT_BYTES = 56 * 1024 * 1024

CONV_HALO = 32
ROUTE_TOKENS = 256
MOE_TILE = 512
MOE_CHUNK = 512
MLSTM_HEADS_PER_STEP = 2

NT_DIMS = (((1,), (1,)), ((), ()))
TN_DIMS = (((0,), (0,)), ((), ()))


def _cparams(n_axes):
    return pltpu.CompilerParams(dimension_semantics=("arbitrary",) * n_axes,
                                vmem_limit_bytes=VMEM_LIMIT_BYTES)


def _row_tile(t, cap=1536):
    best = None
    for cand in range(16, cap + 1, 16):
        if t % cand == 0:
            best = cand
    assert best is not None, t
    return best


def _t5_bucket_starts():
    exact = N_BUCKETS // 2
    d = np.arange(0, 4 * T5_MAX_DIST)
    nf = np.maximum(d, 1).astype(np.float64)
    large = exact + (np.log(nf / exact) / math.log(T5_MAX_DIST / exact) * (N_BUCKETS - exact)).astype(np.int64)
    bucket = np.where(d < exact, d, np.minimum(large, N_BUCKETS - 1))
    starts = [int(d[bucket >= b].min()) for b in range(N_BUCKETS)]
    assert starts[-1] <= MOBA_BLOCK, "blocks two or more back must share the last bucket"
    return starts


def _mm_body(*refs, k_parts, has_init):
    n_x = len(k_parts)
    x_refs, w_ref = refs[:n_x], refs[n_x]
    o_ref, wbf_ref = refs[n_x + 1 + has_init:]

    @pl.when(pl.program_id(1) == 0)
    def _():
        wbf_ref[...] = w_ref[...].astype(BF16)

    acc, k0 = None, 0
    for x_ref, kp in zip(x_refs, k_parts):
        part = jnp.dot(x_ref[...].astype(BF16), wbf_ref[k0:k0 + kp, :], preferred_element_type=F32)
        acc = part if acc is None else acc + part
        k0 += kp
    o_ref[...] = acc.astype(o_ref.dtype)


def matmul(xs, w, *, lead=None, col0=0, n, tn, out_dtype, name, tm=None, out_rows=None, row_block0=0,
           out_init=None):
    m = xs[0].shape[0]
    k_parts = tuple(x.shape[1] for x in xs)
    k = sum(k_parts)
    tm = tm or _row_tile(m)
    out_rows = out_rows or m
    assert m % tm == 0 and n % tn == 0 and col0 % tn == 0 and w.shape[-2] == k
    jb = col0 // tn
    if lead is None:
        w_spec = pl.BlockSpec((k, tn), lambda j, i: (0, j + jb))
    else:
        w_spec = pl.BlockSpec((None, k, tn), lambda j, i: (lead, 0, j + jb))
    in_specs = [pl.BlockSpec((tm, kp), lambda j, i: (i, 0)) for kp in k_parts] + [w_spec]
    args = list(xs) + [w]
    aliases = {}
    if out_init is not None:
        in_specs.append(pl.BlockSpec(memory_space=pl.ANY))
        aliases = {len(args): 0}
        args.append(out_init)
    return pl.pallas_call(
        functools.partial(_mm_body, k_parts=k_parts, has_init=int(out_init is not None)),
        grid=(n // tn, m // tm),
        in_specs=in_specs,
        out_specs=pl.BlockSpec((tm, tn), lambda j, i: (row_block0 + i, j)),
        out_shape=jax.ShapeDtypeStruct((out_rows, n), out_dtype),
        scratch_shapes=[pltpu.VMEM((k, tn), BF16)],
        input_output_aliases=aliases,
        compiler_params=_cparams(2),
        name=name,
    )(*args)


def _layer_norm_rows(z, g, b):
    mu = jnp.mean(z, axis=-1, keepdims=True)
    zc = z - mu
    var = jnp.mean(zc * zc, axis=-1, keepdims=True)
    return zc * lax.rsqrt(var + LN_EPS) * g + b


def _ln_body(x_ref, f_ref, g_ref, b_ref, o_ref, *, alpha):
    o_ref[...] = _layer_norm_rows(alpha * x_ref[...] + f_ref[...], g_ref[...], b_ref[...])


def ln_residual(x, f, g, b, alpha, name):
    t, d = x.shape
    tm = _row_tile(t, 768)
    row = pl.BlockSpec((tm, d), lambda i: (i, 0))
    vec = pl.BlockSpec((1, d), lambda i: (0, 0))
    return pl.pallas_call(
        functools.partial(_ln_body, alpha=alpha),
        grid=(t // tm,),
        in_specs=[row, row, vec, vec],
        out_specs=row,
        out_shape=jax.ShapeDtypeStruct((t, d), F32),
        compiler_params=_cparams(1),
        name=name,
    )(x, f, g.reshape(1, d), b.reshape(1, d))


def _mlstm_body(bg_ref, q_ref, k_ref, v_ref, g_ref, og_ref, gh_ref, c0_ref, n0_ref, m0_ref,
                h_ref, c_ref, n_ref, m_ref, *, chunk, n_chunks, scale, hp, dk, dv):
    L = chunk
    for hh in range(hp):
        c_ref[0, hh] = c0_ref[0, hh]
        n_ref[0, hh] = n0_ref[0, hh]
        m_ref[0, hh] = m0_ref[0, hh]
    row = lax.broadcasted_iota(I32, (L, L), 0)
    col = lax.broadcasted_iota(I32, (L, L), 1)
    causal = col <= row
    eye = col == row
    lane = lax.broadcasted_iota(I32, (L, LANES), 1)

    def one_chunk(r0):
        for hh in range(hp):
            head_chunk(r0, hh)

    def head_chunk(r0, hh):
        head = pl.program_id(1) * hp + hh
        b_i = bg_ref[head]
        b_f = bg_ref[M_HEADS + head]
        qk_cols = slice(hh * dk, (hh + 1) * dk)
        v_cols = slice(hh * dv, (hh + 1) * dv)
        rows = pl.ds(r0, L)
        g = g_ref[rows, :]
        li = jnp.sum(jnp.where(lane == head, g, 0.0), axis=1, keepdims=True) + b_i
        fg = jnp.sum(jnp.where(lane == head + M_HEADS, g, 0.0), axis=1, keepdims=True) + b_f
        lf = -(jnp.maximum(-fg, 0.0) + jnp.log1p(jnp.exp(-jnp.abs(fg))))
        lf_b = jnp.broadcast_to(lf, (L, L))
        li_b = jnp.broadcast_to(li, (L, L))
        lf_row = jnp.sum(jnp.where(eye, lf_b, 0.0), axis=0, keepdims=True)
        li_row = jnp.sum(jnp.where(eye, li_b, 0.0), axis=0, keepdims=True)
        b_col = jnp.sum(jnp.where(causal, jnp.broadcast_to(lf_row, (L, L)), 0.0), axis=1, keepdims=True)
        b_row = jnp.sum(jnp.where(row <= col, lf_b, 0.0), axis=0, keepdims=True)
        m_old = m_ref[0, hh][:, :1]
        dm = jnp.where(causal, b_col - b_row + li_row, NEG_INF)
        inter = b_col + m_old
        mt = jnp.maximum(inter, jnp.max(dm, axis=1, keepdims=True))
        w_int = jnp.exp(inter - mt)
        qq = (q_ref[rows, qk_cols] * scale).astype(BF16)
        kk = k_ref[rows, qk_cols]
        kb = kk.astype(BF16)
        vb = v_ref[rows, v_cols].astype(BF16)
        c_old = c_ref[0, hh]
        n_old = n_ref[0, hh]
        a_ts = jnp.exp(dm - mt) * lax.dot_general(qq, kb, NT_DIMS, preferred_element_type=F32)
        num = (w_int * lax.dot_general(qq, c_old.astype(BF16), NT_DIMS, preferred_element_type=F32)
               + jnp.dot(a_ts.astype(BF16), vb, preferred_element_type=F32))
        qn = jnp.sum(qq.astype(F32) * n_old.astype(BF16).astype(F32), axis=1, keepdims=True)
        den = w_int * qn + jnp.sum(a_ts, axis=1, keepdims=True)
        hid = num / jnp.maximum(jnp.abs(den), jnp.exp(-mt))
        g_tot = b_col[L - 1:L, :]
        dec_col = g_tot - b_col + li
        dec_row = g_tot - b_row + li_row
        m_new = jnp.maximum(g_tot + m_old, jnp.max(dec_row, axis=1, keepdims=True))
        a_old = jnp.exp(g_tot + m_old - m_new)
        wk = jnp.exp(dec_col - m_new)
        c_ref[0, hh] = a_old * c_old + lax.dot_general(vb, (kk * wk).astype(BF16), TN_DIMS,
                                                       preferred_element_type=F32)
        n_ref[0, hh] = a_old * n_old + jnp.sum(kb.astype(F32) * wk.astype(BF16).astype(F32), axis=0, keepdims=True)
        m_ref[0, hh] = jnp.broadcast_to(m_new, (1, LANES))
        mu = jnp.mean(hid, axis=-1, keepdims=True)
        hc = hid - mu
        var = jnp.mean(hc * hc, axis=-1, keepdims=True)
        hn = hc * lax.rsqrt(var + LN_EPS) * gh_ref[:, v_cols]
        h_ref[rows, v_cols] = (hn * jax.nn.sigmoid(og_ref[rows, v_cols])).astype(h_ref.dtype)

    if n_chunks == 1:
        one_chunk(0)
    else:
        def loop_body(c, carry):
            one_chunk(pl.multiple_of(c * L, L))
            return carry
        lax.fori_loop(0, n_chunks, loop_body, 0)


def mlstm(u1, gates, u2, b_gates, g_h, c0, n0, m0, *, seq, row_block0, out_dtype, name):
    bn, heads, dv, dk = c0.shape
    chunk = math.gcd(seq, M_CHUNK)
    n_chunks = seq // chunk
    kv_blk = (2 * heads * dk) // dv
    assert (2 * heads * dk) % dv == 0
    m0b = jnp.broadcast_to(m0.astype(F32)[:, :, None, None], (bn, heads, 1, LANES))
    hp = MLSTM_HEADS_PER_STEP
    assert heads % hp == 0
    body = functools.partial(_mlstm_body, chunk=chunk, n_chunks=n_chunks, scale=dk ** -0.5, hp=hp, dk=dk, dv=dv)
    st4 = lambda b, h: (b, h, 0, 0)
    h, c, n, m = pl.pallas_call(
        body,
        grid=(bn, heads // hp),
        in_specs=[
            pl.BlockSpec(memory_space=pltpu.SMEM),
            pl.BlockSpec((seq, hp * dk), lambda b, h: (row_block0 + b, h)),
            pl.BlockSpec((seq, hp * dk), lambda b, h: (row_block0 + b, heads // hp + h)),
            pl.BlockSpec((seq, hp * dv), lambda b, h: (row_block0 + b, kv_blk // hp + h)),
            pl.BlockSpec((seq, LANES), lambda b, h: (row_block0 + b, 0)),
            pl.BlockSpec((seq, hp * dv), lambda b, h: (row_block0 + b, h)),
            pl.BlockSpec((1, hp * dv), lambda b, h: (0, h)),
            pl.BlockSpec((1, hp, dv, dk), st4),
            pl.BlockSpec((1, hp, 1, dk), st4),
            pl.BlockSpec((1, hp, 1, LANES), st4),
        ],
        out_specs=[
            pl.BlockSpec((seq, hp * dv), lambda b, h: (b, h)),
            pl.BlockSpec((1, hp, dv, dk), st4),
            pl.BlockSpec((1, hp, 1, dk), st4),
            pl.BlockSpec((1, hp, 1, LANES), st4),
        ],
        out_shape=[
            jax.ShapeDtypeStruct((bn * seq, heads * dv), out_dtype),
            jax.ShapeDtypeStruct((bn, heads, dv, dk), F32),
            jax.ShapeDtypeStruct((bn, heads, 1, dk), F32),
            jax.ShapeDtypeStruct((bn, heads, 1, LANES), F32),
        ],
        compiler_params=_cparams(2),
        name=name,
    )(b_gates, u1, u1, u1, gates, u2, g_h.reshape(1, heads * dv),
      c0.astype(F32), n0.astype(F32).reshape(bn, heads, 1, dk), m0b)
    return h, c, n[:, :, 0, :], m[:, :, 0, 0]


def _conv_body(ga_ref, gb_ref, buf0_ref, w_ref, bdw_ref, gcn_ref, bcn_ref, c_ref, nb_ref, win_ref,
               *, ts, n_t, row_group):
    t = pl.program_id(1)
    w1 = CONV_WIDTH - 1
    lead = CONV_HALO - w1

    @pl.when(t == 0)
    def _():
        win_ref[0:lead, :] = jnp.zeros((lead, win_ref.shape[1]), F32)
        win_ref[lead:CONV_HALO, :] = buf0_ref[0]

    if n_t > 1:
        @pl.when(t > 0)
        def _():
            win_ref[0:CONV_HALO, :] = win_ref[ts:ts + CONV_HALO, :]

    win_ref[CONV_HALO:CONV_HALO + ts, :] = ga_ref[...] * jax.nn.sigmoid(gb_ref[...])
    for r0 in range(0, ts, row_group):
        acc = jnp.broadcast_to(bdw_ref[...], (row_group, win_ref.shape[1]))
        for j in range(CONV_WIDTH):
            acc = acc + w_ref[j:j + 1, :] * win_ref[r0 + lead + j:r0 + lead + j + row_group, :]
        y = _layer_norm_rows(acc, gcn_ref[...], bcn_ref[...])
        c_ref[r0:r0 + row_group, :] = (y * jax.nn.sigmoid(y)).astype(c_ref.dtype)

    @pl.when(t == n_t - 1)
    def _():
        nb_ref[0] = win_ref[ts + lead:ts + CONV_HALO, :]


def conv_module(u2, buf0, w_dw, b_dw, g_cn, b_cn, *, seq, ts, row_block0, col_block0, out_dtype, name):
    bn, w1, ch = buf0.shape
    assert w1 == CONV_WIDTH - 1 and seq % ts == 0
    n_t = seq // ts
    row_group = min(ts, 32)
    body = functools.partial(_conv_body, ts=ts, n_t=n_t, row_group=row_group)
    vec = pl.BlockSpec((1, ch), lambda b, t: (0, 0))
    c, nb = pl.pallas_call(
        body,
        grid=(bn, n_t),
        in_specs=[
            pl.BlockSpec((ts, ch), lambda b, t: (row_block0 + b * n_t + t, col_block0)),
            pl.BlockSpec((ts, ch), lambda b, t: (row_block0 + b * n_t + t, col_block0 + 1)),
            pl.BlockSpec((1, w1, ch), lambda b, t: (b, 0, 0)),
            pl.BlockSpec((CONV_WIDTH, ch), lambda b, t: (0, 0)),
            vec, vec, vec,
        ],
        out_specs=[
            pl.BlockSpec((ts, ch), lambda b, t: (b * n_t + t, 0)),
            pl.BlockSpec((1, w1, ch), lambda b, t: (b, 0, 0)),
        ],
        out_shape=[
            jax.ShapeDtypeStruct((bn * seq, ch), out_dtype),
            jax.ShapeDtypeStruct((bn, w1, ch), F32),
        ],
        scratch_shapes=[pltpu.VMEM((CONV_HALO + ts, ch), F32)],
        compiler_params=_cparams(2),
        name=name,
    )(u2, u2, buf0, w_dw, b_dw.reshape(1, ch), g_cn.reshape(1, ch), b_cn.reshape(1, ch))
    return c, nb


def _bias_body(tbl_ref, o_ref, *, starts):
    head = pl.program_id(0)
    blk = o_ref.shape[-1]
    r = lax.broadcasted_iota(I32, (blk, blk), 0)
    c = lax.broadcasted_iota(I32, (blk, blk), 1)
    for which in range(4):
        qk = r - c if which < 2 else c - r
        d = jnp.maximum(qk + (which % 2) * blk, 0)
        val = jnp.full((blk, blk), tbl_ref[0, head], F32)
        for b in range(1, N_BUCKETS):
            val = jnp.where(d >= starts[b], tbl_ref[b, head], val)
        o_ref[0, which] = val


def bias_tiles(rel_bias):
    heads = rel_bias.shape[1]
    return pl.pallas_call(
        functools.partial(_bias_body, starts=_t5_bucket_starts()),
        grid=(heads,),
        in_specs=[pl.BlockSpec(memory_space=pltpu.SMEM)],
        out_specs=pl.BlockSpec((1, 4, MOBA_BLOCK, MOBA_BLOCK), lambda h: (h, 0, 0, 0)),
        out_shape=jax.ShapeDtypeStruct((heads, 4, MOBA_BLOCK, MOBA_BLOCK), F32),
        compiler_params=_cparams(1),
        name="t5_bias_tiles",
    )(rel_bias.astype(F32))


def _moba_body(far_ref, q_ref, k_ref, v_ref, bias_ref, *rest, nb, scale, has_init):
    o_ref, ko_ref, vo_ref, means_ref, kbf_ref, vt_ref = rest[2 * has_init:]
    blk = MOBA_BLOCK
    head = pl.program_id(1)
    i = pl.program_id(2)

    @pl.when(i == 0)
    def _():
        for j in range(nb):
            rows = slice(j * blk, (j + 1) * blk)
            means_ref[j:j + 1, :] = jnp.mean(k_ref[rows, :], axis=0, keepdims=True)
            vt_ref[:, rows] = v_ref[rows, :].T.astype(BF16)
        kbf_ref[...] = k_ref[...].astype(BF16)
        ko_ref[...] = k_ref[...]
        vo_ref[...] = v_ref[...]

    key = lax.broadcasted_iota(I32, (blk, blk), 0)
    qry = lax.broadcasted_iota(I32, (blk, blk), 1)
    far = far_ref[head]

    def query_block(ii):
        qt = q_ref[...].T.astype(BF16)
        if ii > MOBA_TOPK:
            sc = jnp.dot(means_ref[...].astype(BF16), qt, preferred_element_type=F32)
            jio = lax.broadcasted_iota(I32, (nb, blk), 0)
            rank = jnp.zeros((nb, blk), F32)
            for n in range(ii):
                sn = sc[n:n + 1, :]
                beats = (sn > sc) | ((sn == sc) & (n < jio))
                rank = rank + jnp.where(beats, 1.0, 0.0)
            keep = jnp.where(rank < MOBA_TOPK, 1.0, 0.0)
        n_keys = (ii + 1) * blk
        lg = jnp.dot(kbf_ref[0:n_keys, :], qt, preferred_element_type=F32) * scale
        pieces = []
        for j in range(ii + 1):
            piece = lg[j * blk:(j + 1) * blk, :]
            if j == ii:
                piece = jnp.where(key <= qry, piece + bias_ref[0, 0], NEG_INF)
            else:
                piece = piece + (bias_ref[0, 1] if j == ii - 1 else far)
                if ii > MOBA_TOPK:
                    piece = jnp.where(keep[j:j + 1, :] > 0.0, piece, NEG_INF)
            pieces.append(piece)
        m = functools.reduce(jnp.maximum, [jnp.max(p, axis=0, keepdims=True) for p in pieces])
        probs = [jnp.exp(p - m) for p in pieces]
        l = functools.reduce(jnp.add, [jnp.sum(p, axis=0, keepdims=True) for p in probs])
        pcat = jnp.concatenate([p.astype(BF16) for p in probs], axis=0)
        acc = jnp.dot(vt_ref[:, 0:n_keys], pcat, preferred_element_type=F32)
        o_ref[...] = (acc / l).T.astype(o_ref.dtype)

    for ii in range(nb):
        pl.when(i == ii)(functools.partial(query_block, ii))


def moba_prompt(qkv, bias, far, *, batch, seq, layer, n_layers, kv_init, name):
    heads = A_HEADS
    dh = qkv.shape[1] // (3 * heads)
    assert seq % MOBA_BLOCK == 0
    nb = seq // MOBA_BLOCK
    has_init = int(kv_init is not None)
    body = functools.partial(_moba_body, nb=nb, scale=dh ** -0.5, has_init=has_init)
    kv_spec = pl.BlockSpec((None, None, None, seq, dh), lambda b, h, i: (layer, b, h, 0, 0))
    kv_shape = jax.ShapeDtypeStruct((n_layers, batch, heads, seq, dh), F32)
    in_specs = [
        pl.BlockSpec(memory_space=pltpu.SMEM),
        pl.BlockSpec((MOBA_BLOCK, dh), lambda b, h, i: (b * nb + i, h)),
        pl.BlockSpec((seq, dh), lambda b, h, i: (b, heads + h)),
        pl.BlockSpec((seq, dh), lambda b, h, i: (b, 2 * heads + h)),
        pl.BlockSpec((1, 2, MOBA_BLOCK, MOBA_BLOCK), lambda b, h, i: (h, 1, 0, 0)),
    ]
    args = [far, qkv, qkv, qkv, bias]
    aliases = {}
    if has_init:
        in_specs += [pl.BlockSpec(memory_space=pl.ANY)] * 2
        aliases = {len(args): 1, len(args) + 1: 2}
        args += list(kv_init)
    return pl.pallas_call(
        body,
        grid=(batch, heads, nb),
        in_specs=in_specs,
        out_specs=[pl.BlockSpec((MOBA_BLOCK, dh), lambda b, h, i: (b * nb + i, h)), kv_spec, kv_spec],
        out_shape=[jax.ShapeDtypeStruct((batch * seq, heads * dh), BF16), kv_shape, kv_shape],
        scratch_shapes=[pltpu.VMEM((nb, dh), F32), pltpu.VMEM((seq, dh), BF16), pltpu.VMEM((dh, seq), BF16)],
        input_output_aliases=aliases,
        compiler_params=_cparams(3),
        name=name,
    )(*args)


def _page_sum_body(pt_ref, *refs):
    del pt_ref
    page_refs, o_ref = refs[:-1], refs[-1]
    n_layers, heads = page_refs[0].shape[1], page_refs[0].shape[2]
    for a in range(n_layers):
        for h in range(heads):
            parts = [jnp.sum(ck_ref[0, a, h], axis=0, keepdims=True) for ck_ref in page_refs]
            o_ref[0, 0, a, h:h + 1, :] = functools.reduce(jnp.add, parts)


def block_key_sums(cache_k, page_table, n_full, ppb):
    _, n_layers, heads, page, dh = cache_k.shape
    db = page_table.shape[0]

    def page_spec(p):
        return pl.BlockSpec((1, n_layers, heads, page, dh), lambda b, n, pt: (pt[b, n * ppb + p], 0, 0, 0, 0))

    grid_spec = pltpu.PrefetchScalarGridSpec(
        num_scalar_prefetch=1,
        grid=(db, n_full),
        in_specs=[page_spec(p) for p in range(ppb)],
        out_specs=pl.BlockSpec((1, 1, n_layers, heads, dh), lambda b, n, pt: (b, n, 0, 0, 0)),
    )
    return pl.pallas_call(
        _page_sum_body,
        grid_spec=grid_spec,
        out_shape=jax.ShapeDtypeStruct((db, n_full, n_layers, heads, dh), F32),
        compiler_params=_cparams(2),
        name="block_key_sums",
    )(page_table, *([cache_k] * ppb))


def _sample_select_body(q_ref, ksum_ref, o_ref):
    heads, sn, _ = q_ref.shape[1:]
    n_full = ksum_ref.shape[2]
    lane = lax.broadcasted_iota(I32, (sn, n_full), 1).astype(F32)
    out_lane = lax.broadcasted_iota(I32, (sn, LANES), 1)
    for h in range(heads):
        means = (ksum_ref[0, h] * (1.0 / MOBA_BLOCK)).astype(BF16)
        sc = lax.dot_general(q_ref[0, h].astype(BF16), means, NT_DIMS, preferred_element_type=F32)
        out = jnp.zeros((sn, LANES), I32)
        for s in range(min(MOBA_TOPK, n_full)):
            mx = jnp.max(sc, axis=1, keepdims=True)
            idx = jnp.min(jnp.where(sc == mx, lane, float(n_full)), axis=1, keepdims=True)
            out = jnp.where(out_lane == s, idx.astype(I32), out)
            sc = jnp.where(lane == idx, -jnp.inf, sc)
        o_ref[0, h] = out


def sample_select(q, ksum):
    db, heads, sn, dh = q.shape
    n_full = ksum.shape[2]
    return pl.pallas_call(
        _sample_select_body,
        grid=(db,),
        in_specs=[pl.BlockSpec((1, heads, sn, dh), lambda b: (b, 0, 0, 0)),
                  pl.BlockSpec((1, heads, n_full, dh), lambda b: (b, 0, 0, 0))],
        out_specs=pl.BlockSpec((1, heads, sn, LANES), lambda b: (b, 0, 0, 0)),
        out_shape=jax.ShapeDtypeStruct((db, heads, sn, LANES), I32),
        compiler_params=_cparams(1),
        name="moba_sample_select",
    )(q, ksum)


def _sample_attend_body(pages_ref, blks_ref, far_ref, q_ref, kn_ref, vn_ref, bias_ref, ck_hbm, cv_hbm,
                        o_ref, kbuf, vbuf, sem, *, layer, heads, n_full, ppb, scale):
    head = pl.program_id(1)
    step = pl.program_id(0) * heads + head
    n_steps = pl.num_programs(0) * heads
    cur = step % 2
    sn = q_ref.shape[2]
    n_slots = kbuf.shape[2]
    ksel = n_slots // ppb
    base = step * sn

    def copies(g, half, t, s):
        pg = pages_ref[(g * sn + t) * n_slots + s]
        hd = g % heads
        return (pltpu.make_async_copy(ck_hbm.at[pg, layer, hd], kbuf.at[half, t, s], sem.at[half, 0]),
                pltpu.make_async_copy(cv_hbm.at[pg, layer, hd], vbuf.at[half, t, s], sem.at[half, 1]))

    def start_all(g, half):
        for t in range(sn):
            for s in range(n_slots):
                ck, cv = copies(g, half, t, s)
                ck.start()
                cv.start()

    @pl.when(step == 0)
    def _():
        start_all(0, 0)

    @pl.when(step + 1 < n_steps)
    def _():
        start_all(step + 1, 1 - cur)

    for t in range(sn):
        for s in range(n_slots):
            ck, cv = copies(step, cur, t, s)
            ck.wait()
            cv.wait()

    far = far_ref[head]
    width = ppb * PAGE_SIZE
    kn = kn_ref[0, 0].astype(BF16)
    vn = vn_ref[0, 0].astype(BF16)
    own_col = lax.broadcasted_iota(I32, (1, sn), 1)
    for t in range(sn):
        qt = q_ref[0, 0, t:t + 1, :].astype(BF16)
        lg_own = lax.dot_general(qt, kn, NT_DIMS, preferred_element_type=F32) * scale + bias_ref[0, 0, t:t + 1, :sn]
        lg_own = jnp.where(own_col <= t, lg_own, NEG_INF)
        logits = []
        for s in range(ksel):
            kt = kbuf[cur, t, s * ppb:(s + 1) * ppb].reshape(width, kbuf.shape[-1]).astype(BF16)
            blk = blks_ref[(base + t) * ksel + s]
            newest = jnp.full((1, width), blk, I32) == n_full - 1
            bias = jnp.where(newest, bias_ref[0, 1, t:t + 1, :], far)
            logits.append(lax.dot_general(qt, kt, NT_DIMS, preferred_element_type=F32) * scale + bias)
        mx = jnp.max(lg_own, axis=1, keepdims=True)
        for lg in logits:
            mx = jnp.maximum(mx, jnp.max(lg, axis=1, keepdims=True))
        p_own = jnp.exp(lg_own - mx)
        den = jnp.sum(p_own, axis=1, keepdims=True)
        acc = jnp.dot(p_own.astype(BF16), vn, preferred_element_type=F32)
        for s, lg in enumerate(logits):
            p = jnp.exp(lg - mx)
            den = den + jnp.sum(p, axis=1, keepdims=True)
            vt = vbuf[cur, t, s * ppb:(s + 1) * ppb].reshape(width, vbuf.shape[-1]).astype(BF16)
            acc = acc + jnp.dot(p.astype(BF16), vt, preferred_element_type=F32)
        o_ref[0, 0, t:t + 1, :] = acc / den


def sample_attend(q, k_new, v_new, pages, blks, bias, far, cache_k, cache_v, *, layer, n_full, ppb, name):
    db, heads, sn, dh = q.shape
    n_slots = pages.shape[0] // (db * heads * sn)
    body = functools.partial(_sample_attend_body, layer=layer, heads=heads, n_full=n_full, ppb=ppb,
                             scale=dh ** -0.5)
    tok = pl.BlockSpec((1, 1, sn, dh), lambda b, h, *_: (b, h, 0, 0))
    grid_spec = pltpu.PrefetchScalarGridSpec(
        num_scalar_prefetch=2,
        grid=(db, heads),
        in_specs=[
            pl.BlockSpec(memory_space=pltpu.SMEM),
            tok, tok, tok,
            pl.BlockSpec((1, 2, sn, MOBA_BLOCK), lambda b, h, *_: (h, 0, 0, 0)),
            pl.BlockSpec(memory_space=pl.ANY),
            pl.BlockSpec(memory_space=pl.ANY),
        ],
        out_specs=tok,
        scratch_shapes=[
            pltpu.VMEM((2, sn, n_slots, PAGE_SIZE, dh), F32),
            pltpu.VMEM((2, sn, n_slots, PAGE_SIZE, dh), F32),
            pltpu.SemaphoreType.DMA((2, 2)),
        ],
    )
    return pl.pallas_call(
        body,
        grid_spec=grid_spec,
        out_shape=jax.ShapeDtypeStruct((db, heads, sn, dh), F32),
        compiler_params=_cparams(2),
        name=name,
    )(pages, blks, far, q, k_new, v_new, bias, cache_k, cache_v)


def _top2_of(vals):
    n = len(vals)
    m1 = functools.reduce(jnp.maximum, vals)
    i1 = jnp.full(m1.shape, n - 1, I32)
    for k in range(n - 2, -1, -1):
        i1 = jnp.where(vals[k] == m1, k, i1)
    rest = [jnp.where(i1 == k, -1.0, vals[k]) for k in range(n)]
    m2 = functools.reduce(jnp.maximum, rest)
    i2 = jnp.full(m2.shape, n - 1, I32)
    for k in range(n - 2, -1, -1):
        i2 = jnp.where(rest[k] == m2, k, i2)
    return m1, i1, m2, i2


def _route_body(x_ref, wrt_ref, br_ref, e_ref, g_ref, r_ref, cnt_ref, carry_ref, *, n_tokens):
    step = pl.program_id(0)
    tr = x_ref.shape[0]

    @pl.when(step == 0)
    def _():
        carry_ref[...] = jnp.zeros(carry_ref.shape, F32)

    tok = step * tr + lax.broadcasted_iota(I32, (1, tr), 1)
    valid = tok < n_tokens
    logits = lax.dot_general(wrt_ref[...].astype(BF16), x_ref[...].astype(BF16), NT_DIMS,
                             preferred_element_type=F32)
    logits = jnp.where(valid, logits + br_ref[...], 0.0)
    ex = jnp.exp(logits - jnp.max(logits, axis=0, keepdims=True))
    aff = ex / jnp.sum(ex, axis=0, keepdims=True)
    a = [aff[e:e + 1, :] for e in range(N_EXPERTS)]
    group_best = []
    for g in range(N_GROUPS):
        m1, _, m2, _ = _top2_of(a[g * EXP_PER_GROUP:(g + 1) * EXP_PER_GROUP])
        group_best.append(m1 + m2)
    gmax = functools.reduce(jnp.maximum, group_best)
    g_sel = jnp.full(gmax.shape, N_GROUPS - 1, I32)
    for g in range(N_GROUPS - 2, -1, -1):
        g_sel = jnp.where(group_best[g] == gmax, g, g_sel)
    in_g = []
    for k in range(EXP_PER_GROUP):
        v = a[(N_GROUPS - 1) * EXP_PER_GROUP + k]
        for g in range(N_GROUPS - 2, -1, -1):
            v = jnp.where(g_sel == g, a[g * EXP_PER_GROUP + k], v)
        in_g.append(v)
    v1, i1, v2, i2 = _top2_of(in_g)
    e0 = g_sel * EXP_PER_GROUP + i1
    e1 = g_sel * EXP_PER_GROUP + i2
    tot = v1 + v2
    eio = lax.broadcasted_iota(I32, (N_EXPERTS, tr), 0)
    hit0 = (eio == e0) & valid
    hit1 = (eio == e1) & valid
    cnt = jnp.where(hit0 | hit1, 1.0, 0.0)
    s_io = lax.broadcasted_iota(I32, (tr, tr), 0)
    t_io = lax.broadcasted_iota(I32, (tr, tr), 1)
    before = jnp.where(s_io < t_io, 1.0, 0.0).astype(BF16)
    prior = jnp.dot(cnt.astype(BF16), before, preferred_element_type=F32) + carry_ref[:, :1]
    rank0 = jnp.sum(jnp.where(hit0, prior, 0.0), axis=0, keepdims=True)
    rank1 = jnp.sum(jnp.where(hit1, prior, 0.0), axis=0, keepdims=True)
    carry_ref[...] = carry_ref[...] + jnp.sum(cnt, axis=1, keepdims=True)
    sub = lax.broadcasted_iota(I32, (SUBLANES, tr), 0)

    def rows01(r0, r1):
        return jnp.where(sub == 0, r0, jnp.where(sub == 1, r1, jnp.zeros_like(r0)))

    e_ref[...] = rows01(e0, e1)
    r_ref[...] = rows01(rank0.astype(I32), rank1.astype(I32))
    g_ref[...] = rows01(v1 / tot, v2 / tot)
    cnt_ref[...] = carry_ref[...].astype(I32)


def moe_route(x, w_router, b_router, name):
    t, d = x.shape
    tr = ROUTE_TOKENS
    n_steps = pl.cdiv(t, tr)
    tpad = n_steps * tr
    blk = pl.BlockSpec((SUBLANES, tr), lambda i: (0, i))
    return pl.pallas_call(
        functools.partial(_route_body, n_tokens=t),
        grid=(n_steps,),
        in_specs=[pl.BlockSpec((tr, d), lambda i: (i, 0)),
                  pl.BlockSpec((N_EXPERTS, d), lambda i: (0, 0)),
                  pl.BlockSpec((N_EXPERTS, 1), lambda i: (0, 0))],
        out_specs=[blk, blk, blk, pl.BlockSpec((N_EXPERTS, LANES), lambda i: (0, 0))],
        out_shape=[jax.ShapeDtypeStruct((SUBLANES, tpad), I32),
                   jax.ShapeDtypeStruct((SUBLANES, tpad), F32),
                   jax.ShapeDtypeStruct((SUBLANES, tpad), I32),
                   jax.ShapeDtypeStruct((N_EXPERTS, LANES), I32)],
        scratch_shapes=[pltpu.VMEM((N_EXPERTS, LANES), F32)],
        compiler_params=_cparams(1),
        name=name,
    )(x, w_router.T.astype(F32), b_router.astype(F32).reshape(N_EXPERTS, 1))


def _dispatch_body(dest_ref, x_ref, xbuf_in, xbuf, sem, *, tpad):
    del xbuf_in
    tb = x_ref.shape[0]
    t0 = pl.program_id(0) * tb

    def copy(r, k):
        d = dest_ref[k * tpad + t0 + r]
        return pltpu.make_async_copy(x_ref.at[pl.ds(r, 1), :], xbuf.at[pl.ds(d, 1), :], sem)

    def start(r, carry):
        for k in range(TOP_K):
            copy(r, k).start()
        return carry

    def wait(r, carry):
        for k in range(TOP_K):
            copy(r, k).wait()
        return carry

    lax.fori_loop(0, tb, start, 0)
    lax.fori_loop(0, tb, wait, 0)


def moe_dispatch(x, dest, n_rows, tpad, name):
    t, d = x.shape
    tb = _row_tile(t)
    grid_spec = pltpu.PrefetchScalarGridSpec(
        num_scalar_prefetch=1,
        grid=(t // tb,),
        in_specs=[pl.BlockSpec((tb, d), lambda i, dest: (i, 0)),
                  pl.BlockSpec(memory_space=pl.ANY)],
        out_specs=pl.BlockSpec(memory_space=pl.ANY),
        scratch_shapes=[pltpu.SemaphoreType.DMA(())],
    )
    return pl.pallas_call(
        functools.partial(_dispatch_body, tpad=tpad),
        grid_spec=grid_spec,
        out_shape=jax.ShapeDtypeStruct((n_rows, d), x.dtype),
        input_output_aliases={2: 0},
        compiler_params=_cparams(1),
        name=name,
    )(dest, x, jnp.zeros((n_rows, d), x.dtype))


def _experts_body(te_ref, nu_ref, x_ref, w1_ref, w3_ref, w2_ref, o_ref, xbf_ref):
    del te_ref
    i = pl.program_id(0)
    c = pl.program_id(1)

    @pl.when((i < nu_ref[0]) & (c == 0))
    def _():
        xbf_ref[...] = x_ref[...].astype(BF16)

    @pl.when(i < nu_ref[0])
    def _():
        xb = xbf_ref[...]
        h1 = jnp.dot(xb, w1_ref[...].astype(BF16), preferred_element_type=F32)
        h3 = jnp.dot(xb, w3_ref[...].astype(BF16), preferred_element_type=F32)
        hb = (h1 * jax.nn.sigmoid(h1) * h3).astype(BF16)
        y = jnp.dot(hb, w2_ref[...].astype(BF16), preferred_element_type=F32)

        @pl.when(c == 0)
        def _():
            o_ref[...] = y

        @pl.when(c > 0)
        def _():
            o_ref[...] += y

    @pl.when((i >= nu_ref[0]) & (c == 0))
    def _():
        o_ref[...] = jnp.zeros(o_ref.shape, o_ref.dtype)


def moe_experts(xbuf, tile_expert, n_used, w1, w3, w2, layer, name):
    n_rows, d = xbuf.shape
    de = w1.shape[-1]
    tm, tc = MOE_TILE, MOE_CHUNK
    n_tiles, n_chunks = n_rows // tm, de // tc

    def tile(i, nu):
        return jnp.minimum(i, nu[0] - 1)

    def chunk(i, c, nu):
        return jnp.where(i < nu[0], c, n_chunks - 1)

    grid_spec = pltpu.PrefetchScalarGridSpec(
        num_scalar_prefetch=2,
        grid=(n_tiles, n_chunks),
        in_specs=[
            pl.BlockSpec((tm, d), lambda i, c, te, nu: (tile(i, nu), 0)),
            pl.BlockSpec((None, None, d, tc), lambda i, c, te, nu: (layer, te[tile(i, nu)], 0, chunk(i, c, nu))),
            pl.BlockSpec((None, None, d, tc), lambda i, c, te, nu: (layer, te[tile(i, nu)], 0, chunk(i, c, nu))),
            pl.BlockSpec((None, None, tc, d), lambda i, c, te, nu: (layer, te[tile(i, nu)], chunk(i, c, nu), 0)),
        ],
        out_specs=pl.BlockSpec((tm, d), lambda i, c, te, nu: (i, 0)),
        scratch_shapes=[pltpu.VMEM((tm, d), BF16)],
    )
    return pl.pallas_call(
        _experts_body,
        grid_spec=grid_spec,
        out_shape=jax.ShapeDtypeStruct((n_rows, d), F32),
        compiler_params=_cparams(2),
        name=name,
    )(tile_expert, n_used, xbuf, w1, w3, w2)


def _combine_body(dest_ref, x_ref, gate_ref, g_ref, b_ref, ybuf, o_ref, obf_ref, y0_ref, y1_ref, sem,
                  *, tpad, alpha):
    tb = x_ref.shape[0]
    t0 = pl.program_id(0) * tb
    bufs = (y0_ref, y1_ref)

    def copy(r, k):
        d = dest_ref[k * tpad + t0 + r]
        return pltpu.make_async_copy(ybuf.at[pl.ds(d, 1), :], bufs[k].at[pl.ds(r, 1), :], sem)

    def start(r, carry):
        for k in range(TOP_K):
            copy(r, k).start()
        return carry

    def wait(r, carry):
        for k in range(TOP_K):
            copy(r, k).wait()
        return carry

    lax.fori_loop(0, tb, start, 0)
    lax.fori_loop(0, tb, wait, 0)
    gate = gate_ref[...]
    moe = gate[:, 0:1] * y0_ref[...] + gate[:, 1:2] * y1_ref[...]
    y = _layer_norm_rows(alpha * x_ref[...] + moe, g_ref[...], b_ref[...])
    o_ref[...] = y
    obf_ref[...] = y.astype(BF16)


def moe_combine_ln(x, ybuf, dest, gate_cols, g, b, alpha, tpad, name):
    t, d = x.shape
    tb = _row_tile(t, 768)
    row = pl.BlockSpec((tb, d), lambda i, dest: (i, 0))
    vec = pl.BlockSpec((1, d), lambda i, dest: (0, 0))
    grid_spec = pltpu.PrefetchScalarGridSpec(
        num_scalar_prefetch=1,
        grid=(t // tb,),
        in_specs=[row, pl.BlockSpec((tb, TOP_K), lambda i, dest: (i, 0)), vec, vec,
                  pl.BlockSpec(memory_space=pl.ANY)],
        out_specs=[row, row],
        scratch_shapes=[pltpu.VMEM((tb, d), F32), pltpu.VMEM((tb, d), F32), pltpu.SemaphoreType.DMA(())],
    )
    return pl.pallas_call(
        functools.partial(_combine_body, tpad=tpad, alpha=alpha),
        grid_spec=grid_spec,
        out_shape=[jax.ShapeDtypeStruct((t, d), F32), jax.ShapeDtypeStruct((t, d), BF16)],
        compiler_params=_cparams(1),
        name=name,
    )(dest, x, gate_cols, g.reshape(1, d), b.reshape(1, d), ybuf)


def moe_layer(x, w_router, b_router, w1, w3, w2, layer, g, b, alpha):
    t, d = x.shape
    tm = MOE_TILE
    e_idx, gate, rank, counts = moe_route(x, w_router, b_router, name=f"moe_route_{layer}")
    tpad = e_idx.shape[1]
    counts = counts[:, 0]
    padded = (counts + tm - 1) // tm * tm
    pad_end = jnp.cumsum(padded)
    pad_start = pad_end - padded
    is_e = e_idx[:TOP_K, None, :] == jnp.arange(N_EXPERTS, dtype=I32)[None, :, None]
    dest = jnp.sum(jnp.where(is_e, pad_start[None, :, None], 0), axis=1) + rank[:TOP_K]
    dest = dest.reshape(-1).astype(I32)
    n_tiles = (t * TOP_K + N_EXPERTS * (tm - 1)) // tm + 1
    n_used = (pad_end[-1] // tm).astype(I32).reshape(1)
    tile_expert = jnp.minimum(jnp.searchsorted(pad_end, jnp.arange(n_tiles, dtype=I32) * tm, side="right"),
                              N_EXPERTS - 1).astype(I32)
    xbuf = moe_dispatch(x, dest, n_tiles * tm, tpad, name=f"moe_dispatch_{layer}")
    ybuf = moe_experts(xbuf, tile_expert, n_used, w1, w3, w2, layer, name=f"moe_experts_{layer}")
    gate_cols = gate[:TOP_K, :t].T
    return moe_combine_ln(x, ybuf, dest, gate_cols, g, b, alpha, tpad, name=f"moe_combine_{layer}")


def kernel(x_prompt, x_sample, state_mlstm_C, state_mlstm_n, state_mlstm_m, state_conv, cache_k, cache_v, page_table, w_in_rec, b_gate_i, b_gate_f, g_mlstm_norm, w_dw, b_dw, g_conv_norm, b_conv_norm, w_out_rec, w_qkv_attn, w_o_attn, rel_bias, ln_mix_g, ln_mix_b, ln_ffn_g, ln_ffn_b, w_router, b_router, w1_exp, w3_exp, w2_exp):
    bp, seq, d = x_prompt.shape
    db, sn, _ = x_sample.shape
    depth = ln_mix_g.shape[0]
    alpha = (2 * depth) ** 0.25
    tp, ts_ = bp * seq, db * sn
    heads_m, dv, dk = state_mlstm_C.shape[2:]
    ch = state_conv.shape[-1]
    qk_w = 2 * heads_m * dk + heads_m * dv
    n_gate = 2 * heads_m
    rest_w = heads_m * dv + 2 * ch
    assert w_in_rec.shape[-1] == qk_w + n_gate + rest_w and heads_m == M_HEADS
    assert tp % sn == 0 and tp % seq == 0
    dh = d // A_HEADS
    past = page_table.shape[1] * PAGE_SIZE
    ppb = MOBA_BLOCK // PAGE_SIZE
    n_full = past // MOBA_BLOCK
    assert past % MOBA_BLOCK == 0 and n_full >= 1 and sn <= MOBA_BLOCK, "sample keys of the current block must all be new"

    x = jnp.concatenate([x_prompt.reshape(tp, d), x_sample.reshape(ts_, d)], axis=0)
    x_in = x
    sdt = state_mlstm_C.dtype

    bias = bias_tiles(rel_bias)
    far = rel_bias[N_BUCKETS - 1].astype(F32)
    ksums = block_key_sums(cache_k, page_table, n_full, ppb)
    ksums = jnp.transpose(ksums, (2, 0, 3, 1, 4))

    w_rest = w_in_rec[:, :, qk_w + n_gate:]
    w_gate = jnp.pad(w_in_rec[:, :, qk_w:qk_w + n_gate], ((0, 0), (0, 0), (0, LANES - n_gate)))
    z_c = jnp.zeros((bp, heads_m, dv, dk), F32)
    z_n = jnp.zeros((bp, heads_m, dk), F32)
    z_m = jnp.zeros((bp, heads_m), F32)
    z_buf = jnp.zeros((bp, CONV_WIDTH - 1, ch), F32)

    t_all = tp + ts_
    p_tile = _row_tile(tp, 1024)
    assert tp % ts_ == 0
    rec_p, rec_s, kv_s = [], [], []
    kv_stacks = None
    for l in range(depth):
        if l % 2 == 0:
            r = l // 2
            u1 = matmul([x_in], w_in_rec, lead=r, n=qk_w, tn=512, out_dtype=F32, name=f"rec_in_qkv_{r}")
            u2 = matmul([x_in], w_rest, lead=r, n=rest_w, tn=512, out_dtype=F32, name=f"rec_in_rest_{r}")
            gates = matmul([x_in], w_gate, lead=r, n=LANES, tn=LANES, out_dtype=F32, name=f"rec_in_gates_{r}")
            b_g = jnp.concatenate([b_gate_i[r], b_gate_f[r]]).astype(F32)
            h_p, c_p, n_p, m_p = mlstm(u1, gates, u2, b_g, g_mlstm_norm[r], z_c, z_n, z_m,
                                       seq=seq, row_block0=0, out_dtype=BF16, name=f"mlstm_prompt_{r}")
            h_s, c_s, n_s, m_s = mlstm(u1, gates, u2, b_g, g_mlstm_norm[r], state_mlstm_C[r], state_mlstm_n[r],
                                       state_mlstm_m[r], seq=sn, row_block0=tp // sn, out_dtype=F32,
                                       name=f"mlstm_sample_{r}")
            conv_cols = (heads_m * dv) // ch
            assert (heads_m * dv) % ch == 0
            cv_p, buf_p = conv_module(u2, z_buf, w_dw[r], b_dw[r], g_conv_norm[r], b_conv_norm[r], seq=seq, ts=128,
                                      row_block0=0, col_block0=conv_cols, out_dtype=BF16, name=f"conv_prompt_{r}")
            cv_s, buf_s = conv_module(u2, state_conv[r].astype(F32), w_dw[r], b_dw[r], g_conv_norm[r], b_conv_norm[r],
                                      seq=sn, ts=sn, row_block0=tp // sn, col_block0=conv_cols, out_dtype=F32,
                                      name=f"conv_sample_{r}")
            f = matmul([h_p, cv_p], w_out_rec, lead=r, n=d, tn=512, out_dtype=F32, tm=p_tile, out_rows=t_all,
                       name=f"rec_out_{r}")
            f = matmul([h_s, cv_s], w_out_rec, lead=r, n=d, tn=512, out_dtype=F32, tm=ts_, out_rows=t_all,
                       row_block0=tp // ts_, out_init=f, name=f"rec_out_sample_{r}")
            rec_p.append((c_p, n_p, m_p, buf_p))
            rec_s.append((c_s, n_s, m_s, buf_s))
        else:
            a = l // 2
            qkv = matmul([x_in], w_qkv_attn, lead=a, n=3 * d, tn=512, out_dtype=F32, name=f"attn_qkv_{a}")
            o_p, k_stack, v_stack = moba_prompt(qkv, bias, far, batch=bp, seq=seq, layer=a, n_layers=depth // 2,
                                                kv_init=kv_stacks, name=f"moba_prompt_{a}")
            kv_stacks = (k_stack, v_stack)
            qkv_s = qkv[tp:].reshape(db, sn, 3, A_HEADS, dh).transpose(2, 0, 3, 1, 4)
            q_s, k_s, v_s = qkv_s[0], qkv_s[1], qkv_s[2]
            sel = sample_select(q_s, ksums[a])[..., :MOBA_TOPK]
            page_ids = sel[..., None] * ppb + jnp.arange(ppb, dtype=I32)
            pages = jnp.take_along_axis(page_table[:, None, None, :],
                                        page_ids.reshape(db, A_HEADS, sn, MOBA_TOPK * ppb), axis=-1)
            o_s = sample_attend(q_s, k_s, v_s, pages.reshape(-1).astype(I32), sel.reshape(-1), bias, far,
                                cache_k, cache_v, layer=a, n_full=n_full, ppb=ppb, name=f"moba_sample_{a}")
            o_s = o_s.transpose(0, 2, 1, 3).reshape(ts_, d)
            f = matmul([o_p], w_o_attn, lead=a, n=d, tn=512, out_dtype=F32, tm=p_tile, out_rows=t_all,
                       name=f"attn_out_{a}")
            f = matmul([o_s], w_o_attn, lead=a, n=d, tn=512, out_dtype=F32, tm=ts_, out_rows=t_all,
                       row_block0=tp // ts_, out_init=f, name=f"attn_out_sample_{a}")
            kv_s.append((k_s, v_s))
        x = ln_residual(x, f, ln_mix_g[l], ln_mix_b[l], alpha, name=f"ln_mix_{l}")
        x, x_in = moe_layer(x, w_router, b_router, w1_exp, w3_exp, w2_exp, l, ln_ffn_g[l], ln_ffn_b[l], alpha)

    pdt = x_prompt.dtype
    y_p = x[:tp].reshape(bp, seq, d)
    y_s = x[tp:].reshape(db, sn, d)
    return (y_p, y_s,
            jnp.stack([s[0] for s in rec_p]).astype(pdt), jnp.stack([s[1] for s in rec_p]).astype(pdt),
            jnp.stack([s[2] for s in rec_p]).astype(pdt), jnp.stack([s[3] for s in rec_p]),
            kv_stacks[0], kv_stacks[1],
            jnp.stack([s[0] for s in rec_s]).astype(sdt), jnp.stack([s[1] for s in rec_s]).astype(sdt),
            jnp.stack([s[2] for s in rec_s]).astype(sdt), jnp.stack([s[3] for s in rec_s]),
            jnp.stack([t[0] for t in kv_s]), jnp.stack([t[1] for t in kv_s]))
```

```python
import functools
import math

import numpy as np
import jax
import jax.numpy as jnp
from jax import lax
from jax.experimental import pallas as pl
from jax.experimental.pallas import tpu as pltpu

F32 = jnp.float32
BF16 = jnp.bfloat16
I32 = jnp.int32

M_HEADS = 4
M_CHUNK = 64
CONV_WIDTH = 31
A_HEADS = 16
MOBA_BLOCK = 256
MOBA_TOPK = 3
PAGE_SIZE = 128
N_BUCKETS = 32
T5_MAX_DIST = 128
N_EXPERTS = 16
N_GROUPS = 4
EXP_PER_GROUP = N_EXPERTS // N_GROUPS
TOP_K = 2
LN_EPS = 1e-5
NEG_INF = -1e30

LANES = 128
SUBLANES = 8
VMEM_LIMIT_BYTES = 56 * 1024 * 1024

CONV_HALO = 32
CONV_LANE_BLOCK = 256
PROJ_TN = 1024
ROUTE_TOKENS = 256
MOE_TILE = 512
MOE_CHUNK = 512
MLSTM_HEADS_PER_STEP = 2
ROW_DMA_UNROLL = 8

NT_DIMS = (((1,), (1,)), ((), ()))
TN_DIMS = (((0,), (0,)), ((), ()))


def _cparams(n_axes):
    return pltpu.CompilerParams(dimension_semantics=("arbitrary",) * n_axes,
                                vmem_limit_bytes=VMEM_LIMIT_BYTES)


def _row_tile(t, cap=1536):
    best = None
    for cand in range(16, cap + 1, 16):
        if t % cand == 0:
            best = cand
    assert best is not None, t
    return best


def _t5_bucket_starts():
    exact = N_BUCKETS // 2
    d = np.arange(0, 4 * T5_MAX_DIST)
    nf = np.maximum(d, 1).astype(np.float64)
    large = exact + (np.log(nf / exact) / math.log(T5_MAX_DIST / exact) * (N_BUCKETS - exact)).astype(np.int64)
    bucket = np.where(d < exact, d, np.minimum(large, N_BUCKETS - 1))
    starts = [int(d[bucket >= b].min()) for b in range(N_BUCKETS)]
    assert starts[-1] <= MOBA_BLOCK, "blocks two or more back must share the last bucket"
    return starts


def _mm_body(*refs, k_parts, has_init):
    n_x = len(k_parts)
    x_refs, w_ref = refs[:n_x], refs[n_x]
    o_ref, wbf_ref = refs[n_x + 1 + has_init:]

    @pl.when(pl.program_id(1) == 0)
    def _():
        wbf_ref[...] = w_ref[...].astype(BF16)

    acc, k0 = None, 0
    for x_ref, kp in zip(x_refs, k_parts):
        part = jnp.dot(x_ref[...].astype(BF16), wbf_ref[k0:k0 + kp, :], preferred_element_type=F32)
        acc = part if acc is None else acc + part
        k0 += kp
    o_ref[...] = acc.astype(o_ref.dtype)


def matmul(xs, w, *, lead=None, col0=0, n, tn, out_dtype, name, tm=None, out_rows=None, row_block0=0,
           out_init=None):
    m = xs[0].shape[0]
    k_parts = tuple(x.shape[1] for x in xs)
    k = sum(k_parts)
    tm = tm or _row_tile(m)
    out_rows = out_rows or m
    assert m % tm == 0 and n % tn == 0 and col0 % tn == 0 and w.shape[-2] == k
    jb = col0 // tn
    if lead is None:
        w_spec = pl.BlockSpec((k, tn), lambda j, i: (0, j + jb))
    else:
        w_spec = pl.BlockSpec((None, k, tn), lambda j, i: (lead, 0, j + jb))
    in_specs = [pl.BlockSpec((tm, kp), lambda j, i: (i, 0)) for kp in k_parts] + [w_spec]
    args = list(xs) + [w]
    aliases = {}
    if out_init is not None:
        in_specs.append(pl.BlockSpec(memory_space=pl.ANY))
        aliases = {len(args): 0}
        args.append(out_init)
    return pl.pallas_call(
        functools.partial(_mm_body, k_parts=k_parts, has_init=int(out_init is not None)),
        grid=(n // tn, m // tm),
        in_specs=in_specs,
        out_specs=pl.BlockSpec((tm, tn), lambda j, i: (row_block0 + i, j)),
        out_shape=jax.ShapeDtypeStruct((out_rows, n), out_dtype),
        scratch_shapes=[pltpu.VMEM((k, tn), BF16)],
        input_output_aliases=aliases,
        compiler_params=_cparams(2),
        name=name,
    )(*args)


def _layer_norm_rows(z, g, b):
    mu = jnp.mean(z, axis=-1, keepdims=True)
    zc = z - mu
    var = jnp.mean(zc * zc, axis=-1, keepdims=True)
    return zc * lax.rsqrt(var + LN_EPS) * g + b


def _ln_body(x_ref, f_ref, g_ref, b_ref, o_ref, *, alpha):
    o_ref[...] = _layer_norm_rows(alpha * x_ref[...] + f_ref[...], g_ref[...], b_ref[...])


def ln_residual(x, f, g, b, alpha, name):
    t, d = x.shape
    tm = _row_tile(t, 768)
    row = pl.BlockSpec((tm, d), lambda i: (i, 0))
    vec = pl.BlockSpec((1, d), lambda i: (0, 0))
    return pl.pallas_call(
        functools.partial(_ln_body, alpha=alpha),
        grid=(t // tm,),
        in_specs=[row, row, vec, vec],
        out_specs=row,
        out_shape=jax.ShapeDtypeStruct((t, d), F32),
        compiler_params=_cparams(1),
        name=name,
    )(x, f, g.reshape(1, d), b.reshape(1, d))


def _mlstm_body(bg_ref, q_ref, k_ref, v_ref, g_ref, og_ref, gh_ref, c0_ref, n0_ref, m0_ref,
                h_ref, c_ref, n_ref, m_ref, *, chunk, n_chunks, scale, hp, dk, dv):
    L = chunk
    for hh in range(hp):
        c_ref[0, hh] = c0_ref[0, hh]
        n_ref[0, hh] = n0_ref[0, hh]
        m_ref[0, hh] = m0_ref[0, hh]
    row = lax.broadcasted_iota(I32, (L, L), 0)
    col = lax.broadcasted_iota(I32, (L, L), 1)
    causal = col <= row
    eye = col == row
    lane = lax.broadcasted_iota(I32, (L, LANES), 1)

    def one_chunk(r0):
        for hh in range(hp):
            head_chunk(r0, hh)

    def head_chunk(r0, hh):
        head = pl.program_id(1) * hp + hh
        b_i = bg_ref[head]
        b_f = bg_ref[M_HEADS + head]
        qk_cols = slice(hh * dk, (hh + 1) * dk)
        v_cols = slice(hh * dv, (hh + 1) * dv)
        rows = pl.ds(r0, L)
        g = g_ref[rows, :]
        li = jnp.sum(jnp.where(lane == head, g, 0.0), axis=1, keepdims=True) + b_i
        fg = jnp.sum(jnp.where(lane == head + M_HEADS, g, 0.0), axis=1, keepdims=True) + b_f
        lf = -(jnp.maximum(-fg, 0.0) + jnp.log1p(jnp.exp(-jnp.abs(fg))))
        lf_b = jnp.broadcast_to(lf, (L, L))
        li_b = jnp.broadcast_to(li, (L, L))
        lf_row = jnp.sum(jnp.where(eye, lf_b, 0.0), axis=0, keepdims=True)
        li_row = jnp.sum(jnp.where(eye, li_b, 0.0), axis=0, keepdims=True)
        b_col = jnp.sum(jnp.where(causal, jnp.broadcast_to(lf_row, (L, L)), 0.0), axis=1, keepdims=True)
        b_row = jnp.sum(jnp.where(row <= col, lf_b, 0.0), axis=0, keepdims=True)
        m_old = m_ref[0, hh][:, :1]
        dm = jnp.where(causal, b_col - b_row + li_row, NEG_INF)
        inter = b_col + m_old
        mt = jnp.maximum(inter, jnp.max(dm, axis=1, keepdims=True))
        w_int = jnp.exp(inter - mt)
        qq = (q_ref[rows, qk_cols] * scale).astype(BF16)
        kk = k_ref[rows, qk_cols]
        kb = kk.astype(BF16)
        vb = v_ref[rows, v_cols].astype(BF16)
        c_old = c_ref[0, hh]
        n_old = n_ref[0, hh]
        a_ts = jnp.exp(dm - mt) * lax.dot_general(qq, kb, NT_DIMS, preferred_element_type=F32)
        num = (w_int * lax.dot_general(qq, c_old.astype(BF16), NT_DIMS, preferred_element_type=F32)
               + jnp.dot(a_ts.astype(BF16), vb, preferred_element_type=F32))
        qn = jnp.sum(qq.astype(F32) * n_old.astype(BF16).astype(F32), axis=1, keepdims=True)
        den = w_int * qn + jnp.sum(a_ts, axis=1, keepdims=True)
        hid = num / jnp.maximum(jnp.abs(den), jnp.exp(-mt))
        g_tot = b_col[L - 1:L, :]
        dec_col = g_tot - b_col + li
        dec_row = g_tot - b_row + li_row
        m_new = jnp.maximum(g_tot + m_old, jnp.max(dec_row, axis=1, keepdims=True))
        a_old = jnp.exp(g_tot + m_old - m_new)
        wk = jnp.exp(dec_col - m_new)
        c_ref[0, hh] = a_old * c_old + lax.dot_general(vb, (kk * wk).astype(BF16), TN_DIMS,
                                                       preferred_element_type=F32)
        n_ref[0, hh] = a_old * n_old + jnp.sum(kb.astype(F32) * wk.astype(BF16).astype(F32), axis=0, keepdims=True)
        m_ref[0, hh] = jnp.broadcast_to(m_new, (1, LANES))
        mu = jnp.mean(hid, axis=-1, keepdims=True)
        hc = hid - mu
        var = jnp.mean(hc * hc, axis=-1, keepdims=True)
        hn = hc * lax.rsqrt(var + LN_EPS) * gh_ref[:, v_cols]
        h_ref[rows, v_cols] = (hn * jax.nn.sigmoid(og_ref[rows, v_cols])).astype(h_ref.dtype)

    if n_chunks == 1:
        one_chunk(0)
    else:
        def loop_body(c, carry):
            one_chunk(pl.multiple_of(c * L, L))
            return carry
        lax.fori_loop(0, n_chunks, loop_body, 0)


def mlstm(u1, gates, u2, b_gates, g_h, c0, n0, m0, *, seq, row_block0, out_dtype, name):
    bn, heads, dv, dk = c0.shape
    chunk = math.gcd(seq, M_CHUNK)
    n_chunks = seq // chunk
    kv_blk = (2 * heads * dk) // dv
    assert (2 * heads * dk) % dv == 0
    m0b = jnp.broadcast_to(m0.astype(F32)[:, :, None, None], (bn, heads, 1, LANES))
    hp = MLSTM_HEADS_PER_STEP
    assert heads % hp == 0
    body = functools.partial(_mlstm_body, chunk=chunk, n_chunks=n_chunks, scale=dk ** -0.5, hp=hp, dk=dk, dv=dv)
    st4 = lambda b, h: (b, h, 0, 0)
    h, c, n, m = pl.pallas_call(
        body,
        grid=(bn, heads // hp),
        in_specs=[
            pl.BlockSpec(memory_space=pltpu.SMEM),
            pl.BlockSpec((seq, hp * dk), lambda b, h: (row_block0 + b, h)),
            pl.BlockSpec((seq, hp * dk), lambda b, h: (row_block0 + b, heads // hp + h)),
            pl.BlockSpec((seq, hp * dv), lambda b, h: (row_block0 + b, kv_blk // hp + h)),
            pl.BlockSpec((seq, LANES), lambda b, h: (row_block0 + b, 0)),
            pl.BlockSpec((seq, hp * dv), lambda b, h: (row_block0 + b, h)),
            pl.BlockSpec((1, hp * dv), lambda b, h: (0, h)),
            pl.BlockSpec((1, hp, dv, dk), st4),
            pl.BlockSpec((1, hp, 1, dk), st4),
            pl.BlockSpec((1, hp, 1, LANES), st4),
        ],
        out_specs=[
            pl.BlockSpec((seq, hp * dv), lambda b, h: (b, h)),
            pl.BlockSpec((1, hp, dv, dk), st4),
            pl.BlockSpec((1, hp, 1, dk), st4),
            pl.BlockSpec((1, hp, 1, LANES), st4),
        ],
        out_shape=[
            jax.ShapeDtypeStruct((bn * seq, heads * dv), out_dtype),
            jax.ShapeDtypeStruct((bn, heads, dv, dk), F32),
            jax.ShapeDtypeStruct((bn, heads, 1, dk), F32),
            jax.ShapeDtypeStruct((bn, heads, 1, LANES), F32),
        ],
        compiler_params=_cparams(2),
        name=name,
    )(b_gates, u1, u1, u1, gates, u2, g_h.reshape(1, heads * dv),
      c0.astype(F32), n0.astype(F32).reshape(bn, heads, 1, dk), m0b)
    return h, c, n[:, :, 0, :], m[:, :, 0, 0]


def _conv_body(ga_ref, gb_ref, buf0_ref, w_ref, bdw_ref, gcn_ref, bcn_ref, c_ref, nb_ref, win_ref, pre_ref,
               *, ts, n_t, row_group):
    t = pl.program_id(1)
    w1 = CONV_WIDTH - 1
    lead = CONV_HALO - w1
    ch = win_ref.shape[1]

    @pl.when(t == 0)
    def _():
        win_ref[0:lead, :] = jnp.zeros((lead, ch), F32)
        win_ref[lead:CONV_HALO, :] = buf0_ref[0]
        win_ref[CONV_HALO + ts:, :] = jnp.zeros((SUBLANES, ch), F32)

    if n_t > 1:
        @pl.when(t > 0)
        def _():
            win_ref[0:CONV_HALO, :] = win_ref[ts:ts + CONV_HALO, :]

    win_ref[CONV_HALO:CONV_HALO + ts, :] = ga_ref[...] * jax.nn.sigmoid(gb_ref[...])
    span = row_group + SUBLANES
    for r0 in range(0, ts, row_group):
        for c0 in range(0, ch, CONV_LANE_BLOCK):
            cols = slice(c0, c0 + CONV_LANE_BLOCK)
            acc = jnp.broadcast_to(bdw_ref[:, cols], (row_group, CONV_LANE_BLOCK))
            for s in range(SUBLANES):
                part = None
                for j in range(CONV_WIDTH):
                    if (j + lead) % SUBLANES != s:
                        continue
                    base = r0 + (j + lead) // SUBLANES * SUBLANES
                    term = w_ref[j:j + 1, cols] * win_ref[base:base + span, cols]
                    part = term if part is None else part + term
                if part is not None:
                    acc = acc + part[s:s + row_group, :]
            pre_ref[r0:r0 + row_group, cols] = acc
        y = _layer_norm_rows(pre_ref[r0:r0 + row_group, :], gcn_ref[...], bcn_ref[...])
        c_ref[r0:r0 + row_group, :] = (y * jax.nn.sigmoid(y)).astype(c_ref.dtype)

    @pl.when(t == n_t - 1)
    def _():
        nb_ref[0] = win_ref[ts + lead:ts + CONV_HALO, :]


def conv_module(u2, buf0, w_dw, b_dw, g_cn, b_cn, *, seq, ts, row_block0, col_block0, out_dtype, name):
    bn, w1, ch = buf0.shape
    assert w1 == CONV_WIDTH - 1 and seq % ts == 0
    n_t = seq // ts
    row_group = min(ts, 32)
    body = functools.partial(_conv_body, ts=ts, n_t=n_t, row_group=row_group)
    vec = pl.BlockSpec((1, ch), lambda b, t: (0, 0))
    c, nb = pl.pallas_call(
        body,
        grid=(bn, n_t),
        in_specs=[
            pl.BlockSpec((ts, ch), lambda b, t: (row_block0 + b * n_t + t, col_block0)),
            pl.BlockSpec((ts, ch), lambda b, t: (row_block0 + b * n_t + t, col_block0 + 1)),
            pl.BlockSpec((1, w1, ch), lambda b, t: (b, 0, 0)),
            pl.BlockSpec((CONV_WIDTH, ch), lambda b, t: (0, 0)),
            vec, vec, vec,
        ],
        out_specs=[
            pl.BlockSpec((ts, ch), lambda b, t: (b * n_t + t, 0)),
            pl.BlockSpec((1, w1, ch), lambda b, t: (b, 0, 0)),
        ],
        out_shape=[
            jax.ShapeDtypeStruct((bn * seq, ch), out_dtype),
            jax.ShapeDtypeStruct((bn, w1, ch), F32),
        ],
        scratch_shapes=[pltpu.VMEM((CONV_HALO + ts + SUBLANES, ch), F32), pltpu.VMEM((ts, ch), F32)],
        compiler_params=_cparams(2),
        name=name,
    )(u2, u2, buf0, w_dw, b_dw.reshape(1, ch), g_cn.reshape(1, ch), b_cn.reshape(1, ch))
    return c, nb


def _bias_body(tbl_ref, o_ref, *, starts):
    head = pl.program_id(0)
    blk = o_ref.shape[-1]
    r = lax.broadcasted_iota(I32, (blk, blk), 0)
    c = lax.broadcasted_iota(I32, (blk, blk), 1)
    for which in range(4):
        qk = r - c if which < 2 else c - r
        d = jnp.maximum(qk + (which % 2) * blk, 0)
        val = jnp.full((blk, blk), tbl_ref[0, head], F32)
        for b in range(1, N_BUCKETS):
            val = jnp.where(d >= starts[b], tbl_ref[b, head], val)
        o_ref[0, which] = val


def bias_tiles(rel_bias):
    heads = rel_bias.shape[1]
    return pl.pallas_call(
        functools.partial(_bias_body, starts=_t5_bucket_starts()),
        grid=(heads,),
        in_specs=[pl.BlockSpec(memory_space=pltpu.SMEM)],
        out_specs=pl.BlockSpec((1, 4, MOBA_BLOCK, MOBA_BLOCK), lambda h: (h, 0, 0, 0)),
        out_shape=jax.ShapeDtypeStruct((heads, 4, MOBA_BLOCK, MOBA_BLOCK), F32),
        compiler_params=_cparams(1),
        name="t5_bias_tiles",
    )(rel_bias.astype(F32))


def _moba_body(far_ref, q_ref, k_ref, v_ref, bias_ref, *rest, nb, scale, has_init):
    o_ref, ko_ref, vo_ref, means_ref, kbf_ref, vt_ref = rest[2 * has_init:]
    blk = MOBA_BLOCK
    head = pl.program_id(1)
    i = pl.program_id(2)

    @pl.when(i == 0)
    def _():
        for j in range(nb):
            rows = slice(j * blk, (j + 1) * blk)
            means_ref[j:j + 1, :] = jnp.mean(k_ref[rows, :], axis=0, keepdims=True)
            vt_ref[:, rows] = v_ref[rows, :].T.astype(BF16)
        kbf_ref[...] = k_ref[...].astype(BF16)
        ko_ref[...] = k_ref[...]
        vo_ref[...] = v_ref[...]

    key = lax.broadcasted_iota(I32, (blk, blk), 0)
    qry = lax.broadcasted_iota(I32, (blk, blk), 1)
    far = far_ref[head]

    def query_block(ii):
        qt = q_ref[...].T.astype(BF16)
        if ii > MOBA_TOPK:
            sc = jnp.dot(means_ref[...].astype(BF16), qt, preferred_element_type=F32)
            jio = lax.broadcasted_iota(I32, (nb, blk), 0)
            rank = jnp.zeros((nb, blk), F32)
            for n in range(ii):
                sn = sc[n:n + 1, :]
                beats = (sn > sc) | ((sn == sc) & (n < jio))
                rank = rank + jnp.where(beats, 1.0, 0.0)
            keep = jnp.where(rank < MOBA_TOPK, 1.0, 0.0)
        n_keys = (ii + 1) * blk
        lg = jnp.dot(kbf_ref[0:n_keys, :], qt, preferred_element_type=F32) * scale
        pieces = []
        for j in range(ii + 1):
            piece = lg[j * blk:(j + 1) * blk, :]
            if j == ii:
                piece = jnp.where(key <= qry, piece + bias_ref[0, 0], NEG_INF)
            else:
                piece = piece + (bias_ref[0, 1] if j == ii - 1 else far)
                if ii > MOBA_TOPK:
                    piece = jnp.where(keep[j:j + 1, :] > 0.0, piece, NEG_INF)
            pieces.append(piece)
        m = functools.reduce(jnp.maximum, [jnp.max(p, axis=0, keepdims=True) for p in pieces])
        probs = [jnp.exp(p - m) for p in pieces]
        l = functools.reduce(jnp.add, [jnp.sum(p, axis=0, keepdims=True) for p in probs])
        pcat = jnp.concatenate([p.astype(BF16) for p in probs], axis=0)
        acc = jnp.dot(vt_ref[:, 0:n_keys], pcat, preferred_element_type=F32)
        o_ref[...] = (acc / l).T.astype(o_ref.dtype)

    for ii in range(nb):
        pl.when(i == ii)(functools.partial(query_block, ii))


def moba_prompt(qkv, bias, far, *, batch, seq, layer, n_layers, kv_init, name):
    heads = A_HEADS
    dh = qkv.shape[1] // (3 * heads)
    assert seq % MOBA_BLOCK == 0
    nb = seq // MOBA_BLOCK
    has_init = int(kv_init is not None)
    body = functools.partial(_moba_body, nb=nb, scale=dh ** -0.5, has_init=has_init)
    kv_spec = pl.BlockSpec((None, None, None, seq, dh), lambda b, h, i: (layer, b, h, 0, 0))
    kv_shape = jax.ShapeDtypeStruct((n_layers, batch, heads, seq, dh), F32)
    in_specs = [
        pl.BlockSpec(memory_space=pltpu.SMEM),
        pl.BlockSpec((MOBA_BLOCK, dh), lambda b, h, i: (b * nb + i, h)),
        pl.BlockSpec((seq, dh), lambda b, h, i: (b, heads + h)),
        pl.BlockSpec((seq, dh), lambda b, h, i: (b, 2 * heads + h)),
        pl.BlockSpec((1, 2, MOBA_BLOCK, MOBA_BLOCK), lambda b, h, i: (h, 1, 0, 0)),
    ]
    args = [far, qkv, qkv, qkv, bias]
    aliases = {}
    if has_init:
        in_specs += [pl.BlockSpec(memory_space=pl.ANY)] * 2
        aliases = {len(args): 1, len(args) + 1: 2}
        args += list(kv_init)
    return pl.pallas_call(
        body,
        grid=(batch, heads, nb),
        in_specs=in_specs,
        out_specs=[pl.BlockSpec((MOBA_BLOCK, dh), lambda b, h, i: (b * nb + i, h)), kv_spec, kv_spec],
        out_shape=[jax.ShapeDtypeStruct((batch * seq, heads * dh), BF16), kv_shape, kv_shape],
        scratch_shapes=[pltpu.VMEM((nb, dh), F32), pltpu.VMEM((seq, dh), BF16), pltpu.VMEM((dh, seq), BF16)],
        input_output_aliases=aliases,
        compiler_params=_cparams(3),
        name=name,
    )(*args)


def _page_sum_body(pt_ref, *refs):
    del pt_ref
    page_refs, o_ref = refs[:-1], refs[-1]
    n_layers, heads = page_refs[0].shape[1], page_refs[0].shape[2]
    for a in range(n_layers):
        for h in range(heads):
            parts = [jnp.sum(ck_ref[0, a, h], axis=0, keepdims=True) for ck_ref in page_refs]
            o_ref[0, 0, a, h:h + 1, :] = functools.reduce(jnp.add, parts)


def block_key_sums(cache_k, page_table, n_full, ppb):
    _, n_layers, heads, page, dh = cache_k.shape
    db = page_table.shape[0]

    def page_spec(p):
        return pl.BlockSpec((1, n_layers, heads, page, dh), lambda b, n, pt: (pt[b, n * ppb + p], 0, 0, 0, 0))

    grid_spec = pltpu.PrefetchScalarGridSpec(
        num_scalar_prefetch=1,
        grid=(db, n_full),
        in_specs=[page_spec(p) for p in range(ppb)],
        out_specs=pl.BlockSpec((1, 1, n_layers, heads, dh), lambda b, n, pt: (b, n, 0, 0, 0)),
    )
    return pl.pallas_call(
        _page_sum_body,
        grid_spec=grid_spec,
        out_shape=jax.ShapeDtypeStruct((db, n_full, n_layers, heads, dh), F32),
        compiler_params=_cparams(2),
        name="block_key_sums",
    )(page_table, *([cache_k] * ppb))


def _sample_select_body(q_ref, ksum_ref, o_ref):
    heads, sn, _ = q_ref.shape[1:]
    n_full = ksum_ref.shape[2]
    lane = lax.broadcasted_iota(I32, (sn, n_full), 1).astype(F32)
    out_lane = lax.broadcasted_iota(I32, (sn, LANES), 1)
    for h in range(heads):
        means = (ksum_ref[0, h] * (1.0 / MOBA_BLOCK)).astype(BF16)
        sc = lax.dot_general(q_ref[0, h].astype(BF16), means, NT_DIMS, preferred_element_type=F32)
        out = jnp.zeros((sn, LANES), I32)
        for s in range(min(MOBA_TOPK, n_full)):
            mx = jnp.max(sc, axis=1, keepdims=True)
            idx = jnp.min(jnp.where(sc == mx, lane, float(n_full)), axis=1, keepdims=True)
            out = jnp.where(out_lane == s, idx.astype(I32), out)
            sc = jnp.where(lane == idx, -jnp.inf, sc)
        o_ref[0, h] = out


def sample_select(q, ksum):
    db, heads, sn, dh = q.shape
    n_full = ksum.shape[2]
    return pl.pallas_call(
        _sample_select_body,
        grid=(db,),
        in_specs=[pl.BlockSpec((1, heads, sn, dh), lambda b: (b, 0, 0, 0)),
                  pl.BlockSpec((1, heads, n_full, dh), lambda b: (b, 0, 0, 0))],
        out_specs=pl.BlockSpec((1, heads, sn, LANES), lambda b: (b, 0, 0, 0)),
        out_shape=jax.ShapeDtypeStruct((db, heads, sn, LANES), I32),
        compiler_params=_cparams(1),
        name="moba_sample_select",
    )(q, ksum)


def _sample_attend_body(pages_ref, blks_ref, far_ref, q_ref, kn_ref, vn_ref, bias_ref, ck_hbm, cv_hbm,
                        o_ref, kbuf, vbuf, sem, *, layer, heads, n_full, ppb, scale):
    head = pl.program_id(1)
    step = pl.program_id(0) * heads + head
    n_steps = pl.num_programs(0) * heads
    cur = step % 2
    sn = q_ref.shape[2]
    n_slots = kbuf.shape[2]
    ksel = n_slots // ppb
    base = step * sn

    def copies(g, half, t, s):
        pg = pages_ref[(g * sn + t) * n_slots + s]
        hd = g % heads
        return (pltpu.make_async_copy(ck_hbm.at[pg, layer, hd], kbuf.at[half, t, s], sem.at[half, 0]),
                pltpu.make_async_copy(cv_hbm.at[pg, layer, hd], vbuf.at[half, t, s], sem.at[half, 1]))

    def start_all(g, half):
        for t in range(sn):
            for s in range(n_slots):
                ck, cv = copies(g, half, t, s)
                ck.start()
                cv.start()

    @pl.when(step == 0)
    def _():
        start_all(0, 0)

    @pl.when(step + 1 < n_steps)
    def _():
        start_all(step + 1, 1 - cur)

    for t in range(sn):
        for s in range(n_slots):
            ck, cv = copies(step, cur, t, s)
            ck.wait()
            cv.wait()

    far = far_ref[head]
    width = ppb * PAGE_SIZE
    kn = kn_ref[0, 0].astype(BF16)
    vn = vn_ref[0, 0].astype(BF16)
    own_col = lax.broadcasted_iota(I32, (1, sn), 1)
    for t in range(sn):
        qt = q_ref[0, 0, t:t + 1, :].astype(BF16)
        lg_own = lax.dot_general(qt, kn, NT_DIMS, preferred_element_type=F32) * scale + bias_ref[0, 0, t:t + 1, :sn]
        lg_own = jnp.where(own_col <= t, lg_own, NEG_INF)
        logits = []
        for s in range(ksel):
            kt = kbuf[cur, t, s * ppb:(s + 1) * ppb].reshape(width, kbuf.shape[-1]).astype(BF16)
            blk = blks_ref[(base + t) * ksel + s]
            newest = jnp.full((1, width), blk, I32) == n_full - 1
            bias = jnp.where(newest, bias_ref[0, 1, t:t + 1, :], far)
            logits.append(lax.dot_general(qt, kt, NT_DIMS, preferred_element_type=F32) * scale + bias)
        mx = jnp.max(lg_own, axis=1, keepdims=True)
        for lg in logits:
            mx = jnp.maximum(mx, jnp.max(lg, axis=1, keepdims=True))
        p_own = jnp.exp(lg_own - mx)
        den = jnp.sum(p_own, axis=1, keepdims=True)
        acc = jnp.dot(p_own.astype(BF16), vn, preferred_element_type=F32)
        for s, lg in enumerate(logits):
            p = jnp.exp(lg - mx)
            den = den + jnp.sum(p, axis=1, keepdims=True)
            vt = vbuf[cur, t, s * ppb:(s + 1) * ppb].reshape(width, vbuf.shape[-1]).astype(BF16)
            acc = acc + jnp.dot(p.astype(BF16), vt, preferred_element_type=F32)
        o_ref[0, 0, t:t + 1, :] = acc / den


def sample_attend(q, k_new, v_new, pages, blks, bias, far, cache_k, cache_v, *, layer, n_full, ppb, name):
    db, heads, sn, dh = q.shape
    n_slots = pages.shape[0] // (db * heads * sn)
    body = functools.partial(_sample_attend_body, layer=layer, heads=heads, n_full=n_full, ppb=ppb,
                             scale=dh ** -0.5)
    tok = pl.BlockSpec((1, 1, sn, dh), lambda b, h, *_: (b, h, 0, 0))
    grid_spec = pltpu.PrefetchScalarGridSpec(
        num_scalar_prefetch=2,
        grid=(db, heads),
        in_specs=[
            pl.BlockSpec(memory_space=pltpu.SMEM),
            tok, tok, tok,
            pl.BlockSpec((1, 2, sn, MOBA_BLOCK), lambda b, h, *_: (h, 0, 0, 0)),
            pl.BlockSpec(memory_space=pl.ANY),
            pl.BlockSpec(memory_space=pl.ANY),
        ],
        out_specs=tok,
        scratch_shapes=[
            pltpu.VMEM((2, sn, n_slots, PAGE_SIZE, dh), F32),
            pltpu.VMEM((2, sn, n_slots, PAGE_SIZE, dh), F32),
            pltpu.SemaphoreType.DMA((2, 2)),
        ],
    )
    return pl.pallas_call(
        body,
        grid_spec=grid_spec,
        out_shape=jax.ShapeDtypeStruct((db, heads, sn, dh), F32),
        compiler_params=_cparams(2),
        name=name,
    )(pages, blks, far, q, k_new, v_new, bias, cache_k, cache_v)


def _top2_of(vals):
    n = len(vals)
    m1 = functools.reduce(jnp.maximum, vals)
    i1 = jnp.full(m1.shape, n - 1, I32)
    for k in range(n - 2, -1, -1):
        i1 = jnp.where(vals[k] == m1, k, i1)
    rest = [jnp.where(i1 == k, -1.0, vals[k]) for k in range(n)]
    m2 = functools.reduce(jnp.maximum, rest)
    i2 = jnp.full(m2.shape, n - 1, I32)
    for k in range(n - 2, -1, -1):
        i2 = jnp.where(rest[k] == m2, k, i2)
    return m1, i1, m2, i2


def _route_body(x_ref, wrt_ref, br_ref, e_ref, g_ref, r_ref, cnt_ref, carry_ref, *, n_tokens):
    step = pl.program_id(0)
    tr = x_ref.shape[0]

    @pl.when(step == 0)
    def _():
        carry_ref[...] = jnp.zeros(carry_ref.shape, F32)

    tok = step * tr + lax.broadcasted_iota(I32, (1, tr), 1)
    valid = tok < n_tokens
    logits = lax.dot_general(wrt_ref[...].astype(BF16), x_ref[...].astype(BF16), NT_DIMS,
                             preferred_element_type=F32)
    logits = jnp.where(valid, logits + br_ref[...], 0.0)
    ex = jnp.exp(logits - jnp.max(logits, axis=0, keepdims=True))
    aff = ex / jnp.sum(ex, axis=0, keepdims=True)
    a = [aff[e:e + 1, :] for e in range(N_EXPERTS)]
    group_best = []
    for g in range(N_GROUPS):
        m1, _, m2, _ = _top2_of(a[g * EXP_PER_GROUP:(g + 1) * EXP_PER_GROUP])
        group_best.append(m1 + m2)
    gmax = functools.reduce(jnp.maximum, group_best)
    g_sel = jnp.full(gmax.shape, N_GROUPS - 1, I32)
    for g in range(N_GROUPS - 2, -1, -1):
        g_sel = jnp.where(group_best[g] == gmax, g, g_sel)
    in_g = []
    for k in range(EXP_PER_GROUP):
        v = a[(N_GROUPS - 1) * EXP_PER_GROUP + k]
        for g in range(N_GROUPS - 2, -1, -1):
            v = jnp.where(g_sel == g, a[g * EXP_PER_GROUP + k], v)
        in_g.append(v)
    v1, i1, v2, i2 = _top2_of(in_g)
    e0 = g_sel * EXP_PER_GROUP + i1
    e1 = g_sel * EXP_PER_GROUP + i2
    tot = v1 + v2
    eio = lax.broadcasted_iota(I32, (N_EXPERTS, tr), 0)
    hit0 = (eio == e0) & valid
    hit1 = (eio == e1) & valid
    cnt = jnp.where(hit0 | hit1, 1.0, 0.0)
    s_io = lax.broadcasted_iota(I32, (tr, tr), 0)
    t_io = lax.broadcasted_iota(I32, (tr, tr), 1)
    before = jnp.where(s_io < t_io, 1.0, 0.0).astype(BF16)
    prior = jnp.dot(cnt.astype(BF16), before, preferred_element_type=F32) + carry_ref[:, :1]
    rank0 = jnp.sum(jnp.where(hit0, prior, 0.0), axis=0, keepdims=True)
    rank1 = jnp.sum(jnp.where(hit1, prior, 0.0), axis=0, keepdims=True)
    carry_ref[...] = carry_ref[...] + jnp.sum(cnt, axis=1, keepdims=True)
    sub = lax.broadcasted_iota(I32, (SUBLANES, tr), 0)

    def rows01(r0, r1):
        return jnp.where(sub == 0, r0, jnp.where(sub == 1, r1, jnp.zeros_like(r0)))

    e_ref[...] = rows01(e0, e1)
    r_ref[...] = rows01(rank0.astype(I32), rank1.astype(I32))
    g_ref[...] = rows01(v1 / tot, v2 / tot)
    cnt_ref[...] = carry_ref[...].astype(I32)


def moe_route(x, w_router, b_router, name):
    t, d = x.shape
    tr = ROUTE_TOKENS
    n_steps = pl.cdiv(t, tr)
    tpad = n_steps * tr
    blk = pl.BlockSpec((SUBLANES, tr), lambda i: (0, i))
    return pl.pallas_call(
        functools.partial(_route_body, n_tokens=t),
        grid=(n_steps,),
        in_specs=[pl.BlockSpec((tr, d), lambda i: (i, 0)),
                  pl.BlockSpec((N_EXPERTS, d), lambda i: (0, 0)),
                  pl.BlockSpec((N_EXPERTS, 1), lambda i: (0, 0))],
        out_specs=[blk, blk, blk, pl.BlockSpec((N_EXPERTS, LANES), lambda i: (0, 0))],
        out_shape=[jax.ShapeDtypeStruct((SUBLANES, tpad), I32),
                   jax.ShapeDtypeStruct((SUBLANES, tpad), F32),
                   jax.ShapeDtypeStruct((SUBLANES, tpad), I32),
                   jax.ShapeDtypeStruct((N_EXPERTS, LANES), I32)],
        scratch_shapes=[pltpu.VMEM((N_EXPERTS, LANES), F32)],
        compiler_params=_cparams(1),
        name=name,
    )(x, w_router.T.astype(F32), b_router.astype(F32).reshape(N_EXPERTS, 1))


def _dispatch_body(dest_ref, x_ref, xbuf_in, xbuf, sem, *, tpad):
    del xbuf_in
    tb = x_ref.shape[0]
    t0 = pl.program_id(0) * tb

    def copy(r, k):
        d = dest_ref[k * tpad + t0 + r]
        return pltpu.make_async_copy(x_ref.at[pl.ds(r, 1), :], xbuf.at[pl.ds(d, 1), :], sem)

    def start(r, carry):
        for k in range(TOP_K):
            copy(r, k).start()
        return carry

    def wait(r, carry):
        for k in range(TOP_K):
            copy(r, k).wait()
        return carry

    lax.fori_loop(0, tb, start, 0, unroll=ROW_DMA_UNROLL)
    lax.fori_loop(0, tb, wait, 0, unroll=ROW_DMA_UNROLL)


def moe_dispatch(x, dest, n_rows, tpad, name):
    t, d = x.shape
    tb = _row_tile(t)
    grid_spec = pltpu.PrefetchScalarGridSpec(
        num_scalar_prefetch=1,
        grid=(t // tb,),
        in_specs=[pl.BlockSpec((tb, d), lambda i, dest: (i, 0)),
                  pl.BlockSpec(memory_space=pl.ANY)],
        out_specs=pl.BlockSpec(memory_space=pl.ANY),
        scratch_shapes=[pltpu.SemaphoreType.DMA(())],
    )
    return pl.pallas_call(
        functools.partial(_dispatch_body, tpad=tpad),
        grid_spec=grid_spec,
        out_shape=jax.ShapeDtypeStruct((n_rows, d), x.dtype),
        input_output_aliases={2: 0},
        compiler_params=_cparams(1),
        name=name,
    )(dest, x, jnp.zeros((n_rows, d), x.dtype))


def _experts_body(te_ref, nu_ref, x_ref, w1_ref, w3_ref, w2_ref, o_ref, xbf_ref):
    del te_ref
    i = pl.program_id(0)
    c = pl.program_id(1)

    @pl.when((i < nu_ref[0]) & (c == 0))
    def _():
        xbf_ref[...] = x_ref[...].astype(BF16)

    @pl.when(i < nu_ref[0])
    def _():
        xb = xbf_ref[...]
        h1 = jnp.dot(xb, w1_ref[...].astype(BF16), preferred_element_type=F32)
        h3 = jnp.dot(xb, w3_ref[...].astype(BF16), preferred_element_type=F32)
        hb = (h1 * jax.nn.sigmoid(h1) * h3).astype(BF16)
        y = jnp.dot(hb, w2_ref[...].astype(BF16), preferred_element_type=F32)

        @pl.when(c == 0)
        def _():
            o_ref[...] = y

        @pl.when(c > 0)
        def _():
            o_ref[...] += y

    @pl.when((i >= nu_ref[0]) & (c == 0))
    def _():
        o_ref[...] = jnp.zeros(o_ref.shape, o_ref.dtype)


def moe_experts(xbuf, tile_expert, n_used, w1, w3, w2, layer, name):
    n_rows, d = xbuf.shape
    de = w1.shape[-1]
    tm, tc = MOE_TILE, MOE_CHUNK
    n_tiles, n_chunks = n_rows // tm, de // tc

    def tile(i, nu):
        return jnp.minimum(i, nu[0] - 1)

    def chunk(i, c, nu):
        return jnp.where(i < nu[0], c, n_chunks - 1)

    grid_spec = pltpu.PrefetchScalarGridSpec(
        num_scalar_prefetch=2,
        grid=(n_tiles, n_chunks),
        in_specs=[
            pl.BlockSpec((tm, d), lambda i, c, te, nu: (tile(i, nu), 0)),
            pl.BlockSpec((None, None, d, tc), lambda i, c, te, nu: (layer, te[tile(i, nu)], 0, chunk(i, c, nu))),
            pl.BlockSpec((None, None, d, tc), lambda i, c, te, nu: (layer, te[tile(i, nu)], 0, chunk(i, c, nu))),
            pl.BlockSpec((None, None, tc, d), lambda i, c, te, nu: (layer, te[tile(i, nu)], chunk(i, c, nu), 0)),
        ],
        out_specs=pl.BlockSpec((tm, d), lambda i, c, te, nu: (i, 0)),
        scratch_shapes=[pltpu.VMEM((tm, d), BF16)],
    )
    return pl.pallas_call(
        _experts_body,
        grid_spec=grid_spec,
        out_shape=jax.ShapeDtypeStruct((n_rows, d), F32),
        compiler_params=_cparams(2),
        name=name,
    )(tile_expert, n_used, xbuf, w1, w3, w2)


def _combine_body(dest_ref, x_ref, gate_ref, g_ref, b_ref, ybuf, o_ref, obf_ref, y0_ref, y1_ref, sem,
                  *, tpad, alpha):
    tb = x_ref.shape[0]
    t0 = pl.program_id(0) * tb
    bufs = (y0_ref, y1_ref)

    def copy(r, k):
        d = dest_ref[k * tpad + t0 + r]
        return pltpu.make_async_copy(ybuf.at[pl.ds(d, 1), :], bufs[k].at[pl.ds(r, 1), :], sem)

    def start(r, carry):
        for k in range(TOP_K):
            copy(r, k).start()
        return carry

    def wait(r, carry):
        for k in range(TOP_K):
            copy(r, k).wait()
        return carry

    lax.fori_loop(0, tb, start, 0, unroll=ROW_DMA_UNROLL)
    lax.fori_loop(0, tb, wait, 0, unroll=ROW_DMA_UNROLL)
    gate = gate_ref[...]
    moe = gate[:, 0:1] * y0_ref[...] + gate[:, 1:2] * y1_ref[...]
    y = _layer_norm_rows(alpha * x_ref[...] + moe, g_ref[...], b_ref[...])
    o_ref[...] = y
    obf_ref[...] = y.astype(BF16)


def moe_combine_ln(x, ybuf, dest, gate_cols, g, b, alpha, tpad, name):
    t, d = x.shape
    tb = _row_tile(t, 768)
    row = pl.BlockSpec((tb, d), lambda i, dest: (i, 0))
    vec = pl.BlockSpec((1, d), lambda i, dest: (0, 0))
    grid_spec = pltpu.PrefetchScalarGridSpec(
        num_scalar_prefetch=1,
        grid=(t // tb,),
        in_specs=[row, pl.BlockSpec((tb, TOP_K), lambda i, dest: (i, 0)), vec, vec,
                  pl.BlockSpec(memory_space=pl.ANY)],
        out_specs=[row, row],
        scratch_shapes=[pltpu.VMEM((tb, d), F32), pltpu.VMEM((tb, d), F32), pltpu.SemaphoreType.DMA(())],
    )
    return pl.pallas_call(
        functools.partial(_combine_body, tpad=tpad, alpha=alpha),
        grid_spec=grid_spec,
        out_shape=[jax.ShapeDtypeStruct((t, d), F32), jax.ShapeDtypeStruct((t, d), BF16)],
        compiler_params=_cparams(1),
        name=name,
    )(dest, x, gate_cols, g.reshape(1, d), b.reshape(1, d), ybuf)


def moe_layer(x, w_router, b_router, w1, w3, w2, layer, g, b, alpha):
    t, d = x.shape
    tm = MOE_TILE
    e_idx, gate, rank, counts = moe_route(x, w_router, b_router, name=f"moe_route_{layer}")
    tpad = e_idx.shape[1]
    counts = counts[:, 0]
    padded = (counts + tm - 1) // tm * tm
    pad_end = jnp.cumsum(padded)
    pad_start = pad_end - padded
    is_e = e_idx[:TOP_K, None, :] == jnp.arange(N_EXPERTS, dtype=I32)[None, :, None]
    dest = jnp.sum(jnp.where(is_e, pad_start[None, :, None], 0), axis=1) + rank[:TOP_K]
    dest = dest.reshape(-1).astype(I32)
    n_tiles = (t * TOP_K + N_EXPERTS * (tm - 1)) // tm + 1
    n_used = (pad_end[-1] // tm).astype(I32).reshape(1)
    tile_expert = jnp.minimum(jnp.searchsorted(pad_end, jnp.arange(n_tiles, dtype=I32) * tm, side="right"),
                              N_EXPERTS - 1).astype(I32)
    xbuf = moe_dispatch(x, dest, n_tiles * tm, tpad, name=f"moe_dispatch_{layer}")
    ybuf = moe_experts(xbuf, tile_expert, n_used, w1, w3, w2, layer, name=f"moe_experts_{layer}")
    gate_cols = gate[:TOP_K, :t].T
    return moe_combine_ln(x, ybuf, dest, gate_cols, g, b, alpha, tpad, name=f"moe_combine_{layer}")


def kernel(x_prompt, x_sample, state_mlstm_C, state_mlstm_n, state_mlstm_m, state_conv, cache_k, cache_v, page_table, w_in_rec, b_gate_i, b_gate_f, g_mlstm_norm, w_dw, b_dw, g_conv_norm, b_conv_norm, w_out_rec, w_qkv_attn, w_o_attn, rel_bias, ln_mix_g, ln_mix_b, ln_ffn_g, ln_ffn_b, w_router, b_router, w1_exp, w3_exp, w2_exp):
    bp, seq, d = x_prompt.shape
    db, sn, _ = x_sample.shape
    depth = ln_mix_g.shape[0]
    alpha = (2 * depth) ** 0.25
    tp, ts_ = bp * seq, db * sn
    heads_m, dv, dk = state_mlstm_C.shape[2:]
    ch = state_conv.shape[-1]
    qk_w = 2 * heads_m * dk + heads_m * dv
    n_gate = 2 * heads_m
    rest_w = heads_m * dv + 2 * ch
    assert w_in_rec.shape[-1] == qk_w + n_gate + rest_w and heads_m == M_HEADS
    assert tp % sn == 0 and tp % seq == 0
    dh = d // A_HEADS
    past = page_table.shape[1] * PAGE_SIZE
    ppb = MOBA_BLOCK // PAGE_SIZE
    n_full = past // MOBA_BLOCK
    assert past % MOBA_BLOCK == 0 and n_full >= 1 and sn <= MOBA_BLOCK, "sample keys of the current block must all be new"

    x = jnp.concatenate([x_prompt.reshape(tp, d), x_sample.reshape(ts_, d)], axis=0)
    x_in = x.astype(BF16)
    sdt = state_mlstm_C.dtype

    bias = bias_tiles(rel_bias)
    far = rel_bias[N_BUCKETS - 1].astype(F32)
    ksums = block_key_sums(cache_k, page_table, n_full, ppb)
    ksums = jnp.transpose(ksums, (2, 0, 3, 1, 4))

    w_rest = w_in_rec[:, :, qk_w + n_gate:]
    w_gate = jnp.pad(w_in_rec[:, :, qk_w:qk_w + n_gate], ((0, 0), (0, 0), (0, LANES - n_gate)))
    z_c = jnp.zeros((bp, heads_m, dv, dk), F32)
    z_n = jnp.zeros((bp, heads_m, dk), F32)
    z_m = jnp.zeros((bp, heads_m), F32)
    z_buf = jnp.zeros((bp, CONV_WIDTH - 1, ch), F32)

    t_all = tp + ts_
    p_tile = _row_tile(tp, 1024)
    assert tp % ts_ == 0
    rec_p, rec_s, kv_s = [], [], []
    kv_stacks = None
    for l in range(depth):
        if l % 2 == 0:
            r = l // 2
            u1 = matmul([x_in], w_in_rec, lead=r, n=qk_w, tn=PROJ_TN, out_dtype=F32, name=f"rec_in_qkv_{r}")
            u2 = matmul([x_in], w_rest, lead=r, n=rest_w, tn=PROJ_TN, out_dtype=F32, name=f"rec_in_rest_{r}")
            gates = matmul([x_in], w_gate, lead=r, n=LANES, tn=LANES, out_dtype=F32, name=f"rec_in_gates_{r}")
            b_g = jnp.concatenate([b_gate_i[r], b_gate_f[r]]).astype(F32)
            h_p, c_p, n_p, m_p = mlstm(u1, gates, u2, b_g, g_mlstm_norm[r], z_c, z_n, z_m,
                                       seq=seq, row_block0=0, out_dtype=BF16, name=f"mlstm_prompt_{r}")
            h_s, c_s, n_s, m_s = mlstm(u1, gates, u2, b_g, g_mlstm_norm[r], state_mlstm_C[r], state_mlstm_n[r],
                                       state_mlstm_m[r], seq=sn, row_block0=tp // sn, out_dtype=F32,
                                       name=f"mlstm_sample_{r}")
            conv_cols = (heads_m * dv) // ch
            assert (heads_m * dv) % ch == 0
            cv_p, buf_p = conv_module(u2, z_buf, w_dw[r], b_dw[r], g_conv_norm[r], b_conv_norm[r], seq=seq, ts=128,
                                      row_block0=0, col_block0=conv_cols, out_dtype=BF16, name=f"conv_prompt_{r}")
            cv_s, buf_s = conv_module(u2, state_conv[r].astype(F32), w_dw[r], b_dw[r], g_conv_norm[r], b_conv_norm[r],
                                      seq=sn, ts=sn, row_block0=tp // sn, col_block0=conv_cols, out_dtype=F32,
                                      name=f"conv_sample_{r}")
            f = matmul([h_p, cv_p], w_out_rec, lead=r, n=d, tn=PROJ_TN, out_dtype=F32, tm=p_tile, out_rows=t_all,
                       name=f"rec_out_{r}")
            f = matmul([h_s, cv_s], w_out_rec, lead=r, n=d, tn=PROJ_TN, out_dtype=F32, tm=ts_, out_rows=t_all,
                       row_block0=tp // ts_, out_init=f, name=f"rec_out_sample_{r}")
            rec_p.append((c_p, n_p, m_p, buf_p))
            rec_s.append((c_s, n_s, m_s, buf_s))
        else:
            a = l // 2
            qkv = matmul([x_in], w_qkv_attn, lead=a, n=3 * d, tn=PROJ_TN, out_dtype=F32, name=f"attn_qkv_{a}")
            o_p, k_stack, v_stack = moba_prompt(qkv, bias, far, batch=bp, seq=seq, layer=a, n_layers=depth // 2,
                                                kv_init=kv_stacks, name=f"moba_prompt_{a}")
            kv_stacks = (k_stack, v_stack)
            qkv_s = qkv[tp:].reshape(db, sn, 3, A_HEADS, dh).transpose(2, 0, 3, 1, 4)
            q_s, k_s, v_s = qkv_s[0], qkv_s[1], qkv_s[2]
            sel = sample_select(q_s, ksums[a])[..., :MOBA_TOPK]
            page_ids = sel[..., None] * ppb + jnp.arange(ppb, dtype=I32)
            pages = jnp.take_along_axis(page_table[:, None, None, :],
                                        page_ids.reshape(db, A_HEADS, sn, MOBA_TOPK * ppb), axis=-1)
            o_s = sample_attend(q_s, k_s, v_s, pages.reshape(-1).astype(I32), sel.reshape(-1), bias, far,
                                cache_k, cache_v, layer=a, n_full=n_full, ppb=ppb, name=f"moba_sample_{a}")
            o_s = o_s.transpose(0, 2, 1, 3).reshape(ts_, d)
            f = matmul([o_p], w_o_attn, lead=a, n=d, tn=PROJ_TN, out_dtype=F32, tm=p_tile, out_rows=t_all,
                       name=f"attn_out_{a}")
            f = matmul([o_s], w_o_attn, lead=a, n=d, tn=PROJ_TN, out_dtype=F32, tm=ts_, out_rows=t_all,
                       row_block0=tp // ts_, out_init=f, name=f"attn_out_sample_{a}")
            kv_s.append((k_s, v_s))
        x = ln_residual(x, f, ln_mix_g[l], ln_mix_b[l], alpha, name=f"ln_mix_{l}")
        x, x_in = moe_layer(x, w_router, b_router, w1_exp, w3_exp, w2_exp, l, ln_ffn_g[l], ln_ffn_b[l], alpha)

    pdt = x_prompt.dtype
    y_p = x[:tp].reshape(bp, seq, d)
    y_s = x[tp:].reshape(db, sn, d)
    return (y_p, y_s,
            jnp.stack([s[0] for s in rec_p]).astype(pdt), jnp.stack([s[1] for s in rec_p]).astype(pdt),
            jnp.stack([s[2] for s in rec_p]).astype(pdt), jnp.stack([s[3] for s in rec_p]),
            kv_stacks[0], kv_stacks[1],
            jnp.stack([s[0] for s in rec_s]).astype(sdt), jnp.stack([s[1] for s in rec_s]).astype(sdt),
            jnp.stack([s[2] for s in rec_s]).astype(sdt), jnp.stack([s[3] for s in rec_s]),
            jnp.stack([t[0] for t in kv_s]), jnp.stack([t[1] for t in kv_s]))
```

```python
import functools
import math

import numpy as np
import jax
import jax.numpy as jnp
from jax import lax
from jax.experimental import pallas as pl
from jax.experimental.pallas import tpu as pltpu

F32 = jnp.float32
BF16 = jnp.bfloat16
I32 = jnp.int32

M_HEADS = 4
M_CHUNK = 64
CONV_WIDTH = 31
A_HEADS = 16
MOBA_BLOCK = 256
MOBA_TOPK = 3
PAGE_SIZE = 128
N_BUCKETS = 32
T5_MAX_DIST = 128
N_EXPERTS = 16
N_GROUPS = 4
EXP_PER_GROUP = N_EXPERTS // N_GROUPS
TOP_K = 2
LN_EPS = 1e-5
NEG_INF = -1e30

LANES = 128
SUBLANES = 8
VMEM_LIMIT_BYTES = 56 * 1024 * 1024

CONV_HALO = 32
CONV_LANE_BLOCK = 256
PROJ_TN = 1024
ROUTE_TOKENS = 256
MOE_TILE = 512
MOE_CHUNK = 512
MLSTM_HEADS_PER_STEP = 2
ROW_DMA_UNROLL = 8

NT_DIMS = (((1,), (1,)), ((), ()))
TN_DIMS = (((0,), (0,)), ((), ()))


def _cparams(n_axes):
    return pltpu.CompilerParams(dimension_semantics=("arbitrary",) * n_axes,
                                vmem_limit_bytes=VMEM_LIMIT_BYTES)


def _row_tile(t, cap=1536):
    best = None
    for cand in range(16, cap + 1, 16):
        if t % cand == 0:
            best = cand
    assert best is not None, t
    return best


def _t5_bucket_starts():
    exact = N_BUCKETS // 2
    d = np.arange(0, 4 * T5_MAX_DIST)
    nf = np.maximum(d, 1).astype(np.float64)
    large = exact + (np.log(nf / exact) / math.log(T5_MAX_DIST / exact) * (N_BUCKETS - exact)).astype(np.int64)
    bucket = np.where(d < exact, d, np.minimum(large, N_BUCKETS - 1))
    starts = [int(d[bucket >= b].min()) for b in range(N_BUCKETS)]
    assert starts[-1] <= MOBA_BLOCK, "blocks two or more back must share the last bucket"
    return starts


def _mm_body(*refs, k_parts, has_init):
    n_x = len(k_parts)
    x_refs, w_ref = refs[:n_x], refs[n_x]
    o_ref, wbf_ref = refs[n_x + 1 + has_init:]

    @pl.when(pl.program_id(1) == 0)
    def _():
        wbf_ref[...] = w_ref[...].astype(BF16)

    acc, k0 = None, 0
    for x_ref, kp in zip(x_refs, k_parts):
        part = jnp.dot(x_ref[...].astype(BF16), wbf_ref[k0:k0 + kp, :], preferred_element_type=F32)
        acc = part if acc is None else acc + part
        k0 += kp
    o_ref[...] = acc.astype(o_ref.dtype)


def matmul(xs, w, *, lead=None, col0=0, n, tn, out_dtype, name, tm=None, out_rows=None, row_block0=0,
           out_init=None):
    m = xs[0].shape[0]
    k_parts = tuple(x.shape[1] for x in xs)
    k = sum(k_parts)
    tm = tm or _row_tile(m)
    out_rows = out_rows or m
    assert m % tm == 0 and n % tn == 0 and col0 % tn == 0 and w.shape[-2] == k
    jb = col0 // tn
    if lead is None:
        w_spec = pl.BlockSpec((k, tn), lambda j, i: (0, j + jb))
    else:
        w_spec = pl.BlockSpec((None, k, tn), lambda j, i: (lead, 0, j + jb))
    in_specs = [pl.BlockSpec((tm, kp), lambda j, i: (i, 0)) for kp in k_parts] + [w_spec]
    args = list(xs) + [w]
    aliases = {}
    if out_init is not None:
        in_specs.append(pl.BlockSpec(memory_space=pl.ANY))
        aliases = {len(args): 0}
        args.append(out_init)
    return pl.pallas_call(
        functools.partial(_mm_body, k_parts=k_parts, has_init=int(out_init is not None)),
        grid=(n // tn, m // tm),
        in_specs=in_specs,
        out_specs=pl.BlockSpec((tm, tn), lambda j, i: (row_block0 + i, j)),
        out_shape=jax.ShapeDtypeStruct((out_rows, n), out_dtype),
        scratch_shapes=[pltpu.VMEM((k, tn), BF16)],
        input_output_aliases=aliases,
        compiler_params=_cparams(2),
        name=name,
    )(*args)


def _layer_norm_rows(z, g, b):
    mu = jnp.mean(z, axis=-1, keepdims=True)
    zc = z - mu
    var = jnp.mean(zc * zc, axis=-1, keepdims=True)
    return zc * lax.rsqrt(var + LN_EPS) * g + b


def _ln_body(x_ref, f_ref, g_ref, b_ref, o_ref, *, alpha):
    o_ref[...] = _layer_norm_rows(alpha * x_ref[...] + f_ref[...], g_ref[...], b_ref[...])


def ln_residual(x, f, g, b, alpha, name):
    t, d = x.shape
    tm = _row_tile(t, 768)
    row = pl.BlockSpec((tm, d), lambda i: (i, 0))
    vec = pl.BlockSpec((1, d), lambda i: (0, 0))
    return pl.pallas_call(
        functools.partial(_ln_body, alpha=alpha),
        grid=(t // tm,),
        in_specs=[row, row, vec, vec],
        out_specs=row,
        out_shape=jax.ShapeDtypeStruct((t, d), F32),
        compiler_params=_cparams(1),
        name=name,
    )(x, f, g.reshape(1, d), b.reshape(1, d))


def _mlstm_body(bg_ref, q_ref, k_ref, v_ref, g_ref, og_ref, gh_ref, c0_ref, n0_ref, m0_ref,
                h_ref, c_ref, n_ref, m_ref, *, chunk, n_chunks, scale, hp, dk, dv):
    L = chunk
    for hh in range(hp):
        c_ref[0, hh] = c0_ref[0, hh]
        n_ref[0, hh] = n0_ref[0, hh]
        m_ref[0, hh] = m0_ref[0, hh]
    row = lax.broadcasted_iota(I32, (L, L), 0)
    col = lax.broadcasted_iota(I32, (L, L), 1)
    causal = col <= row
    eye = col == row
    lane = lax.broadcasted_iota(I32, (L, LANES), 1)

    def one_chunk(r0):
        for hh in range(hp):
            head_chunk(r0, hh)

    def head_chunk(r0, hh):
        head = pl.program_id(1) * hp + hh
        b_i = bg_ref[head]
        b_f = bg_ref[M_HEADS + head]
        qk_cols = slice(hh * dk, (hh + 1) * dk)
        v_cols = slice(hh * dv, (hh + 1) * dv)
        rows = pl.ds(r0, L)
        g = g_ref[rows, :]
        li = jnp.sum(jnp.where(lane == head, g, 0.0), axis=1, keepdims=True) + b_i
        fg = jnp.sum(jnp.where(lane == head + M_HEADS, g, 0.0), axis=1, keepdims=True) + b_f
        lf = -(jnp.maximum(-fg, 0.0) + jnp.log1p(jnp.exp(-jnp.abs(fg))))
        lf_b = jnp.broadcast_to(lf, (L, L))
        li_b = jnp.broadcast_to(li, (L, L))
        lf_row = jnp.sum(jnp.where(eye, lf_b, 0.0), axis=0, keepdims=True)
        li_row = jnp.sum(jnp.where(eye, li_b, 0.0), axis=0, keepdims=True)
        b_col = jnp.sum(jnp.where(causal, jnp.broadcast_to(lf_row, (L, L)), 0.0), axis=1, keepdims=True)
        b_row = jnp.sum(jnp.where(row <= col, lf_b, 0.0), axis=0, keepdims=True)
        m_old = m_ref[0, hh][:, :1]
        dm = jnp.where(causal, b_col - b_row + li_row, NEG_INF)
        inter = b_col + m_old
        mt = jnp.maximum(inter, jnp.max(dm, axis=1, keepdims=True))
        w_int = jnp.exp(inter - mt)
        qq = (q_ref[rows, qk_cols] * scale).astype(BF16)
        kk = k_ref[rows, qk_cols]
        kb = kk.astype(BF16)
        vb = v_ref[rows, v_cols].astype(BF16)
        c_old = c_ref[0, hh]
        n_old = n_ref[0, hh]
        a_ts = jnp.exp(dm - mt) * lax.dot_general(qq, kb, NT_DIMS, preferred_element_type=F32)
        num = (w_int * lax.dot_general(qq, c_old.astype(BF16), NT_DIMS, preferred_element_type=F32)
               + jnp.dot(a_ts.astype(BF16), vb, preferred_element_type=F32))
        qn = jnp.sum(qq.astype(F32) * n_old.astype(BF16).astype(F32), axis=1, keepdims=True)
        den = w_int * qn + jnp.sum(a_ts, axis=1, keepdims=True)
        hid = num / jnp.maximum(jnp.abs(den), jnp.exp(-mt))
        g_tot = b_col[L - 1:L, :]
        dec_col = g_tot - b_col + li
        dec_row = g_tot - b_row + li_row
        m_new = jnp.maximum(g_tot + m_old, jnp.max(dec_row, axis=1, keepdims=True))
        a_old = jnp.exp(g_tot + m_old - m_new)
        wk = jnp.exp(dec_col - m_new)
        c_ref[0, hh] = a_old * c_old + lax.dot_general(vb, (kk * wk).astype(BF16), TN_DIMS,
                                                       preferred_element_type=F32)
        n_ref[0, hh] = a_old * n_old + jnp.sum(kb.astype(F32) * wk.astype(BF16).astype(F32), axis=0, keepdims=True)
        m_ref[0, hh] = jnp.broadcast_to(m_new, (1, LANES))
        mu = jnp.mean(hid, axis=-1, keepdims=True)
        hc = hid - mu
        var = jnp.mean(hc * hc, axis=-1, keepdims=True)
        hn = hc * lax.rsqrt(var + LN_EPS) * gh_ref[:, v_cols]
        h_ref[rows, v_cols] = (hn * jax.nn.sigmoid(og_ref[rows, v_cols])).astype(h_ref.dtype)

    if n_chunks == 1:
        one_chunk(0)
    else:
        def loop_body(c, carry):
            one_chunk(pl.multiple_of(c * L, L))
            return carry
        lax.fori_loop(0, n_chunks, loop_body, 0)


def mlstm(u1, gates, u2, b_gates, g_h, c0, n0, m0, *, seq, row_block0, out_dtype, name):
    bn, heads, dv, dk = c0.shape
    chunk = math.gcd(seq, M_CHUNK)
    n_chunks = seq // chunk
    kv_blk = (2 * heads * dk) // dv
    assert (2 * heads * dk) % dv == 0
    m0b = jnp.broadcast_to(m0.astype(F32)[:, :, None, None], (bn, heads, 1, LANES))
    hp = MLSTM_HEADS_PER_STEP
    assert heads % hp == 0
    body = functools.partial(_mlstm_body, chunk=chunk, n_chunks=n_chunks, scale=dk ** -0.5, hp=hp, dk=dk, dv=dv)
    st4 = lambda b, h: (b, h, 0, 0)
    h, c, n, m = pl.pallas_call(
        body,
        grid=(bn, heads // hp),
        in_specs=[
            pl.BlockSpec(memory_space=pltpu.SMEM),
            pl.BlockSpec((seq, hp * dk), lambda b, h: (row_block0 + b, h)),
            pl.BlockSpec((seq, hp * dk), lambda b, h: (row_block0 + b, heads // hp + h)),
            pl.BlockSpec((seq, hp * dv), lambda b, h: (row_block0 + b, kv_blk // hp + h)),
            pl.BlockSpec((seq, LANES), lambda b, h: (row_block0 + b, 0)),
            pl.BlockSpec((seq, hp * dv), lambda b, h: (row_block0 + b, h)),
            pl.BlockSpec((1, hp * dv), lambda b, h: (0, h)),
            pl.BlockSpec((1, hp, dv, dk), st4),
            pl.BlockSpec((1, hp, 1, dk), st4),
            pl.BlockSpec((1, hp, 1, LANES), st4),
        ],
        out_specs=[
            pl.BlockSpec((seq, hp * dv), lambda b, h: (b, h)),
            pl.BlockSpec((1, hp, dv, dk), st4),
            pl.BlockSpec((1, hp, 1, dk), st4),
            pl.BlockSpec((1, hp, 1, LANES), st4),
        ],
        out_shape=[
            jax.ShapeDtypeStruct((bn * seq, heads * dv), out_dtype),
            jax.ShapeDtypeStruct((bn, heads, dv, dk), F32),
            jax.ShapeDtypeStruct((bn, heads, 1, dk), F32),
            jax.ShapeDtypeStruct((bn, heads, 1, LANES), F32),
        ],
        compiler_params=_cparams(2),
        name=name,
    )(b_gates, u1, u1, u1, gates, u2, g_h.reshape(1, heads * dv),
      c0.astype(F32), n0.astype(F32).reshape(bn, heads, 1, dk), m0b)
    return h, c, n[:, :, 0, :], m[:, :, 0, 0]


def _conv_body(ga_ref, gb_ref, buf0_ref, w_ref, bdw_ref, gcn_ref, bcn_ref, c_ref, nb_ref, win_ref, pre_ref,
               *, ts, n_t, row_group):
    t = pl.program_id(1)
    w1 = CONV_WIDTH - 1
    lead = CONV_HALO - w1
    ch = win_ref.shape[1]

    @pl.when(t == 0)
    def _():
        win_ref[0:lead, :] = jnp.zeros((lead, ch), F32)
        win_ref[lead:CONV_HALO, :] = buf0_ref[0]
        win_ref[CONV_HALO + ts:, :] = jnp.zeros((SUBLANES, ch), F32)

    if n_t > 1:
        @pl.when(t > 0)
        def _():
            win_ref[0:CONV_HALO, :] = win_ref[ts:ts + CONV_HALO, :]

    win_ref[CONV_HALO:CONV_HALO + ts, :] = ga_ref[...] * jax.nn.sigmoid(gb_ref[...])
    span = row_group + SUBLANES
    for r0 in range(0, ts, row_group):
        for c0 in range(0, ch, CONV_LANE_BLOCK):
            cols = slice(c0, c0 + CONV_LANE_BLOCK)
            acc = jnp.broadcast_to(bdw_ref[:, cols], (row_group, CONV_LANE_BLOCK))
            for s in range(SUBLANES):
                part = None
                for j in range(CONV_WIDTH):
                    if (j + lead) % SUBLANES != s:
                        continue
                    base = r0 + (j + lead) // SUBLANES * SUBLANES
                    term = w_ref[j:j + 1, cols] * win_ref[base:base + span, cols]
                    part = term if part is None else part + term
                if part is not None:
                    acc = acc + part[s:s + row_group, :]
            pre_ref[r0:r0 + row_group, cols] = acc
        y = _layer_norm_rows(pre_ref[r0:r0 + row_group, :], gcn_ref[...], bcn_ref[...])
        c_ref[r0:r0 + row_group, :] = (y * jax.nn.sigmoid(y)).astype(c_ref.dtype)

    @pl.when(t == n_t - 1)
    def _():
        nb_ref[0] = win_ref[ts + lead:ts + CONV_HALO, :]


def conv_module(u2, buf0, w_dw, b_dw, g_cn, b_cn, *, seq, ts, row_block0, col_block0, out_dtype, name):
    bn, w1, ch = buf0.shape
    assert w1 == CONV_WIDTH - 1 and seq % ts == 0
    n_t = seq // ts
    row_group = min(ts, 32)
    body = functools.partial(_conv_body, ts=ts, n_t=n_t, row_group=row_group)
    vec = pl.BlockSpec((1, ch), lambda b, t: (0, 0))
    c, nb = pl.pallas_call(
        body,
        grid=(bn, n_t),
        in_specs=[
            pl.BlockSpec((ts, ch), lambda b, t: (row_block0 + b * n_t + t, col_block0)),
            pl.BlockSpec((ts, ch), lambda b, t: (row_block0 + b * n_t + t, col_block0 + 1)),
            pl.BlockSpec((1, w1, ch), lambda b, t: (b, 0, 0)),
            pl.BlockSpec((CONV_WIDTH, ch), lambda b, t: (0, 0)),
            vec, vec, vec,
        ],
        out_specs=[
            pl.BlockSpec((ts, ch), lambda b, t: (b * n_t + t, 0)),
            pl.BlockSpec((1, w1, ch), lambda b, t: (b, 0, 0)),
        ],
        out_shape=[
            jax.ShapeDtypeStruct((bn * seq, ch), out_dtype),
            jax.ShapeDtypeStruct((bn, w1, ch), F32),
        ],
        scratch_shapes=[pltpu.VMEM((CONV_HALO + ts + SUBLANES, ch), F32), pltpu.VMEM((ts, ch), F32)],
        compiler_params=_cparams(2),
        name=name,
    )(u2, u2, buf0, w_dw, b_dw.reshape(1, ch), g_cn.reshape(1, ch), b_cn.reshape(1, ch))
    return c, nb


def _bias_body(tbl_ref, o_ref, *, starts):
    head = pl.program_id(0)
    blk = o_ref.shape[-1]
    r = lax.broadcasted_iota(I32, (blk, blk), 0)
    c = lax.broadcasted_iota(I32, (blk, blk), 1)
    for which in range(4):
        qk = r - c if which < 2 else c - r
        d = jnp.maximum(qk + (which % 2) * blk, 0)
        val = jnp.full((blk, blk), tbl_ref[0, head], F32)
        for b in range(1, N_BUCKETS):
            val = jnp.where(d >= starts[b], tbl_ref[b, head], val)
        o_ref[0, which] = val


def bias_tiles(rel_bias):
    heads = rel_bias.shape[1]
    return pl.pallas_call(
        functools.partial(_bias_body, starts=_t5_bucket_starts()),
        grid=(heads,),
        in_specs=[pl.BlockSpec(memory_space=pltpu.SMEM)],
        out_specs=pl.BlockSpec((1, 4, MOBA_BLOCK, MOBA_BLOCK), lambda h: (h, 0, 0, 0)),
        out_shape=jax.ShapeDtypeStruct((heads, 4, MOBA_BLOCK, MOBA_BLOCK), F32),
        compiler_params=_cparams(1),
        name="t5_bias_tiles",
    )(rel_bias.astype(F32))


def _moba_body(far_ref, q_ref, k_ref, v_ref, bias_ref, *rest, nb, scale, has_init):
    o_ref, ko_ref, vo_ref, means_ref, kbf_ref, vt_ref = rest[2 * has_init:]
    blk = MOBA_BLOCK
    head = pl.program_id(1)
    i = pl.program_id(2)

    @pl.when(i == 0)
    def _():
        for j in range(nb):
            rows = slice(j * blk, (j + 1) * blk)
            means_ref[j:j + 1, :] = jnp.mean(k_ref[rows, :], axis=0, keepdims=True)
            vt_ref[:, rows] = v_ref[rows, :].T.astype(BF16)
        kbf_ref[...] = k_ref[...].astype(BF16)
        ko_ref[...] = k_ref[...]
        vo_ref[...] = v_ref[...]

    key = lax.broadcasted_iota(I32, (blk, blk), 0)
    qry = lax.broadcasted_iota(I32, (blk, blk), 1)
    far = far_ref[head]

    def query_block(ii):
        qt = q_ref[...].T.astype(BF16)
        if ii > MOBA_TOPK:
            sc = jnp.dot(means_ref[...].astype(BF16), qt, preferred_element_type=F32)
            jio = lax.broadcasted_iota(I32, (nb, blk), 0)
            rank = jnp.zeros((nb, blk), F32)
            for n in range(ii):
                sn = sc[n:n + 1, :]
                beats = (sn > sc) | ((sn == sc) & (n < jio))
                rank = rank + jnp.where(beats, 1.0, 0.0)
            keep = jnp.where(rank < MOBA_TOPK, 1.0, 0.0)
        n_keys = (ii + 1) * blk
        lg = jnp.dot(kbf_ref[0:n_keys, :], qt, preferred_element_type=F32) * scale
        pieces = []
        for j in range(ii + 1):
            piece = lg[j * blk:(j + 1) * blk, :]
            if j == ii:
                piece = jnp.where(key <= qry, piece + bias_ref[0, 0], NEG_INF)
            else:
                piece = piece + (bias_ref[0, 1] if j == ii - 1 else far)
                if ii > MOBA_TOPK:
                    piece = jnp.where(keep[j:j + 1, :] > 0.0, piece, NEG_INF)
            pieces.append(piece)
        m = functools.reduce(jnp.maximum, [jnp.max(p, axis=0, keepdims=True) for p in pieces])
        probs = [jnp.exp(p - m) for p in pieces]
        l = functools.reduce(jnp.add, [jnp.sum(p, axis=0, keepdims=True) for p in probs])
        pcat = jnp.concatenate([p.astype(BF16) for p in probs], axis=0)
        acc = jnp.dot(vt_ref[:, 0:n_keys], pcat, preferred_element_type=F32)
        o_ref[...] = (acc / l).T.astype(o_ref.dtype)

    for ii in range(nb):
        pl.when(i == ii)(functools.partial(query_block, ii))


def moba_prompt(qkv, bias, far, *, batch, seq, layer, n_layers, kv_init, name):
    heads = A_HEADS
    dh = qkv.shape[1] // (3 * heads)
    assert seq % MOBA_BLOCK == 0
    nb = seq // MOBA_BLOCK
    has_init = int(kv_init is not None)
    body = functools.partial(_moba_body, nb=nb, scale=dh ** -0.5, has_init=has_init)
    kv_spec = pl.BlockSpec((None, None, None, seq, dh), lambda b, h, i: (layer, b, h, 0, 0))
    kv_shape = jax.ShapeDtypeStruct((n_layers, batch, heads, seq, dh), F32)
    in_specs = [
        pl.BlockSpec(memory_space=pltpu.SMEM),
        pl.BlockSpec((MOBA_BLOCK, dh), lambda b, h, i: (b * nb + i, h)),
        pl.BlockSpec((seq, dh), lambda b, h, i: (b, heads + h)),
        pl.BlockSpec((seq, dh), lambda b, h, i: (b, 2 * heads + h)),
        pl.BlockSpec((1, 2, MOBA_BLOCK, MOBA_BLOCK), lambda b, h, i: (h, 1, 0, 0)),
    ]
    args = [far, qkv, qkv, qkv, bias]
    aliases = {}
    if has_init:
        in_specs += [pl.BlockSpec(memory_space=pl.ANY)] * 2
        aliases = {len(args): 1, len(args) + 1: 2}
        args += list(kv_init)
    return pl.pallas_call(
        body,
        grid=(batch, heads, nb),
        in_specs=in_specs,
        out_specs=[pl.BlockSpec((MOBA_BLOCK, dh), lambda b, h, i: (b * nb + i, h)), kv_spec, kv_spec],
        out_shape=[jax.ShapeDtypeStruct((batch * seq, heads * dh), BF16), kv_shape, kv_shape],
        scratch_shapes=[pltpu.VMEM((nb, dh), F32), pltpu.VMEM((seq, dh), BF16), pltpu.VMEM((dh, seq), BF16)],
        input_output_aliases=aliases,
        compiler_params=_cparams(3),
        name=name,
    )(*args)


def _page_sum_body(pt_ref, *refs):
    del pt_ref
    page_refs, o_ref = refs[:-1], refs[-1]
    n_layers, heads = page_refs[0].shape[1], page_refs[0].shape[2]
    for a in range(n_layers):
        for h in range(heads):
            parts = [jnp.sum(ck_ref[0, a, h], axis=0, keepdims=True) for ck_ref in page_refs]
            o_ref[0, 0, a, h:h + 1, :] = functools.reduce(jnp.add, parts)


def block_key_sums(cache_k, page_table, n_full, ppb):
    _, n_layers, heads, page, dh = cache_k.shape
    db = page_table.shape[0]

    def page_spec(p):
        return pl.BlockSpec((1, n_layers, heads, page, dh), lambda b, n, pt: (pt[b, n * ppb + p], 0, 0, 0, 0))

    grid_spec = pltpu.PrefetchScalarGridSpec(
        num_scalar_prefetch=1,
        grid=(db, n_full),
        in_specs=[page_spec(p) for p in range(ppb)],
        out_specs=pl.BlockSpec((1, 1, n_layers, heads, dh), lambda b, n, pt: (b, n, 0, 0, 0)),
    )
    return pl.pallas_call(
        _page_sum_body,
        grid_spec=grid_spec,
        out_shape=jax.ShapeDtypeStruct((db, n_full, n_layers, heads, dh), F32),
        compiler_params=_cparams(2),
        name="block_key_sums",
    )(page_table, *([cache_k] * ppb))


def _sample_select_body(q_ref, ksum_ref, o_ref):
    heads, sn, _ = q_ref.shape[1:]
    n_full = ksum_ref.shape[2]
    lane = lax.broadcasted_iota(I32, (sn, n_full), 1).astype(F32)
    out_lane = lax.broadcasted_iota(I32, (sn, LANES), 1)
    for h in range(heads):
        means = (ksum_ref[0, h] * (1.0 / MOBA_BLOCK)).astype(BF16)
        sc = lax.dot_general(q_ref[0, h].astype(BF16), means, NT_DIMS, preferred_element_type=F32)
        out = jnp.zeros((sn, LANES), I32)
        for s in range(min(MOBA_TOPK, n_full)):
            mx = jnp.max(sc, axis=1, keepdims=True)
            idx = jnp.min(jnp.where(sc == mx, lane, float(n_full)), axis=1, keepdims=True)
            out = jnp.where(out_lane == s, idx.astype(I32), out)
            sc = jnp.where(lane == idx, -jnp.inf, sc)
        o_ref[0, h] = out


def sample_select(q, ksum):
    db, heads, sn, dh = q.shape
    n_full = ksum.shape[2]
    return pl.pallas_call(
        _sample_select_body,
        grid=(db,),
        in_specs=[pl.BlockSpec((1, heads, sn, dh), lambda b: (b, 0, 0, 0)),
                  pl.BlockSpec((1, heads, n_full, dh), lambda b: (b, 0, 0, 0))],
        out_specs=pl.BlockSpec((1, heads, sn, LANES), lambda b: (b, 0, 0, 0)),
        out_shape=jax.ShapeDtypeStruct((db, heads, sn, LANES), I32),
        compiler_params=_cparams(1),
        name="moba_sample_select",
    )(q, ksum)


def _sample_attend_body(pages_ref, blks_ref, far_ref, q_ref, kn_ref, vn_ref, bias_ref, ck_hbm, cv_hbm,
                        o_ref, kbuf, vbuf, sem, *, layer, heads, n_full, ppb, scale):
    head = pl.program_id(1)
    step = pl.program_id(0) * heads + head
    n_steps = pl.num_programs(0) * heads
    cur = step % 2
    sn = q_ref.shape[2]
    n_slots = kbuf.shape[2]
    ksel = n_slots // ppb
    base = step * sn

    def copies(g, half, t, s):
        pg = pages_ref[(g * sn + t) * n_slots + s]
        hd = g % heads
        return (pltpu.make_async_copy(ck_hbm.at[pg, layer, hd], kbuf.at[half, t, s], sem.at[half, 0]),
                pltpu.make_async_copy(cv_hbm.at[pg, layer, hd], vbuf.at[half, t, s], sem.at[half, 1]))

    def start_all(g, half):
        for t in range(sn):
            for s in range(n_slots):
                ck, cv = copies(g, half, t, s)
                ck.start()
                cv.start()

    @pl.when(step == 0)
    def _():
        start_all(0, 0)

    @pl.when(step + 1 < n_steps)
    def _():
        start_all(step + 1, 1 - cur)

    for t in range(sn):
        for s in range(n_slots):
            ck, cv = copies(step, cur, t, s)
            ck.wait()
            cv.wait()

    far = far_ref[head]
    width = ppb * PAGE_SIZE
    kn = kn_ref[0, 0].astype(BF16)
    vn = vn_ref[0, 0].astype(BF16)
    own_col = lax.broadcasted_iota(I32, (1, sn), 1)
    for t in range(sn):
        qt = q_ref[0, 0, t:t + 1, :].astype(BF16)
        lg_own = lax.dot_general(qt, kn, NT_DIMS, preferred_element_type=F32) * scale + bias_ref[0, 0, t:t + 1, :sn]
        lg_own = jnp.where(own_col <= t, lg_own, NEG_INF)
        biases = []
        for s in range(ksel):
            blk = blks_ref[(base + t) * ksel + s]
            newest = jnp.full((1, width), blk, I32) == n_full - 1
            biases.append(jnp.where(newest, bias_ref[0, 1, t:t + 1, :], far))
        kt = kbuf[cur, t].reshape(ksel * width, kbuf.shape[-1]).astype(BF16)
        lg = lax.dot_general(qt, kt, NT_DIMS, preferred_element_type=F32) * scale + jnp.concatenate(biases, axis=1)
        mx = jnp.maximum(jnp.max(lg_own, axis=1, keepdims=True), jnp.max(lg, axis=1, keepdims=True))
        p_own = jnp.exp(lg_own - mx)
        p = jnp.exp(lg - mx)
        den = jnp.sum(p_own, axis=1, keepdims=True) + jnp.sum(p, axis=1, keepdims=True)
        vt = vbuf[cur, t].reshape(ksel * width, vbuf.shape[-1]).astype(BF16)
        acc = (jnp.dot(p_own.astype(BF16), vn, preferred_element_type=F32)
               + jnp.dot(p.astype(BF16), vt, preferred_element_type=F32))
        o_ref[0, 0, t:t + 1, :] = acc / den


def sample_attend(q, k_new, v_new, pages, blks, bias, far, cache_k, cache_v, *, layer, n_full, ppb, name):
    db, heads, sn, dh = q.shape
    n_slots = pages.shape[0] // (db * heads * sn)
    body = functools.partial(_sample_attend_body, layer=layer, heads=heads, n_full=n_full, ppb=ppb,
                             scale=dh ** -0.5)
    tok = pl.BlockSpec((1, 1, sn, dh), lambda b, h, *_: (b, h, 0, 0))
    grid_spec = pltpu.PrefetchScalarGridSpec(
        num_scalar_prefetch=2,
        grid=(db, heads),
        in_specs=[
            pl.BlockSpec(memory_space=pltpu.SMEM),
            tok, tok, tok,
            pl.BlockSpec((1, 2, sn, MOBA_BLOCK), lambda b, h, *_: (h, 0, 0, 0)),
            pl.BlockSpec(memory_space=pl.ANY),
            pl.BlockSpec(memory_space=pl.ANY),
        ],
        out_specs=tok,
        scratch_shapes=[
            pltpu.VMEM((2, sn, n_slots, PAGE_SIZE, dh), F32),
            pltpu.VMEM((2, sn, n_slots, PAGE_SIZE, dh), F32),
            pltpu.SemaphoreType.DMA((2, 2)),
        ],
    )
    return pl.pallas_call(
        body,
        grid_spec=grid_spec,
        out_shape=jax.ShapeDtypeStruct((db, heads, sn, dh), F32),
        compiler_params=_cparams(2),
        name=name,
    )(pages, blks, far, q, k_new, v_new, bias, cache_k, cache_v)


def _top2_of(vals):
    n = len(vals)
    m1 = functools.reduce(jnp.maximum, vals)
    i1 = jnp.full(m1.shape, n - 1, I32)
    for k in range(n - 2, -1, -1):
        i1 = jnp.where(vals[k] == m1, k, i1)
    rest = [jnp.where(i1 == k, -1.0, vals[k]) for k in range(n)]
    m2 = functools.reduce(jnp.maximum, rest)
    i2 = jnp.full(m2.shape, n - 1, I32)
    for k in range(n - 2, -1, -1):
        i2 = jnp.where(rest[k] == m2, k, i2)
    return m1, i1, m2, i2


def _route_body(x_ref, wrt_ref, br_ref, e_ref, g_ref, r_ref, cnt_ref, carry_ref, *, n_tokens):
    step = pl.program_id(0)
    tr = x_ref.shape[0]

    @pl.when(step == 0)
    def _():
        carry_ref[...] = jnp.zeros(carry_ref.shape, F32)

    tok = step * tr + lax.broadcasted_iota(I32, (1, tr), 1)
    valid = tok < n_tokens
    logits = lax.dot_general(wrt_ref[...].astype(BF16), x_ref[...].astype(BF16), NT_DIMS,
                             preferred_element_type=F32)
    logits = jnp.where(valid, logits + br_ref[...], 0.0)
    ex = jnp.exp(logits - jnp.max(logits, axis=0, keepdims=True))
    aff = ex / jnp.sum(ex, axis=0, keepdims=True)
    a = [aff[e:e + 1, :] for e in range(N_EXPERTS)]
    group_best = []
    for g in range(N_GROUPS):
        m1, _, m2, _ = _top2_of(a[g * EXP_PER_GROUP:(g + 1) * EXP_PER_GROUP])
        group_best.append(m1 + m2)
    gmax = functools.reduce(jnp.maximum, group_best)
    g_sel = jnp.full(gmax.shape, N_GROUPS - 1, I32)
    for g in range(N_GROUPS - 2, -1, -1):
        g_sel = jnp.where(group_best[g] == gmax, g, g_sel)
    in_g = []
    for k in range(EXP_PER_GROUP):
        v = a[(N_GROUPS - 1) * EXP_PER_GROUP + k]
        for g in range(N_GROUPS - 2, -1, -1):
            v = jnp.where(g_sel == g, a[g * EXP_PER_GROUP + k], v)
        in_g.append(v)
    v1, i1, v2, i2 = _top2_of(in_g)
    e0 = g_sel * EXP_PER_GROUP + i1
    e1 = g_sel * EXP_PER_GROUP + i2
    tot = v1 + v2
    eio = lax.broadcasted_iota(I32, (N_EXPERTS, tr), 0)
    hit0 = (eio == e0) & valid
    hit1 = (eio == e1) & valid
    cnt = jnp.where(hit0 | hit1, 1.0, 0.0)
    s_io = lax.broadcasted_iota(I32, (tr, tr), 0)
    t_io = lax.broadcasted_iota(I32, (tr, tr), 1)
    before = jnp.where(s_io < t_io, 1.0, 0.0).astype(BF16)
    prior = jnp.dot(cnt.astype(BF16), before, preferred_element_type=F32) + carry_ref[:, :1]
    rank0 = jnp.sum(jnp.where(hit0, prior, 0.0), axis=0, keepdims=True)
    rank1 = jnp.sum(jnp.where(hit1, prior, 0.0), axis=0, keepdims=True)
    carry_ref[...] = carry_ref[...] + jnp.sum(cnt, axis=1, keepdims=True)
    sub = lax.broadcasted_iota(I32, (SUBLANES, tr), 0)

    def rows01(r0, r1):
        return jnp.where(sub == 0, r0, jnp.where(sub == 1, r1, jnp.zeros_like(r0)))

    e_ref[...] = rows01(e0, e1)
    r_ref[...] = rows01(rank0.astype(I32), rank1.astype(I32))
    g_ref[...] = rows01(v1 / tot, v2 / tot)
    cnt_ref[...] = carry_ref[...].astype(I32)


def moe_route(x, w_router, b_router, name):
    t, d = x.shape
    tr = ROUTE_TOKENS
    n_steps = pl.cdiv(t, tr)
    tpad = n_steps * tr
    blk = pl.BlockSpec((SUBLANES, tr), lambda i: (0, i))
    return pl.pallas_call(
        functools.partial(_route_body, n_tokens=t),
        grid=(n_steps,),
        in_specs=[pl.BlockSpec((tr, d), lambda i: (i, 0)),
                  pl.BlockSpec((N_EXPERTS, d), lambda i: (0, 0)),
                  pl.BlockSpec((N_EXPERTS, 1), lambda i: (0, 0))],
        out_specs=[blk, blk, blk, pl.BlockSpec((N_EXPERTS, LANES), lambda i: (0, 0))],
        out_shape=[jax.ShapeDtypeStruct((SUBLANES, tpad), I32),
                   jax.ShapeDtypeStruct((SUBLANES, tpad), F32),
                   jax.ShapeDtypeStruct((SUBLANES, tpad), I32),
                   jax.ShapeDtypeStruct((N_EXPERTS, LANES), I32)],
        scratch_shapes=[pltpu.VMEM((N_EXPERTS, LANES), F32)],
        compiler_params=_cparams(1),
        name=name,
    )(x, w_router.T.astype(F32), b_router.astype(F32).reshape(N_EXPERTS, 1))


def _dispatch_body(dest_ref, x_ref, xbuf_in, xbuf, sem, *, tpad):
    del xbuf_in
    tb = x_ref.shape[0]
    t0 = pl.program_id(0) * tb

    def copy(r, k):
        d = dest_ref[k * tpad + t0 + r]
        return pltpu.make_async_copy(x_ref.at[pl.ds(r, 1), :], xbuf.at[pl.ds(d, 1), :], sem)

    def start(r, carry):
        for k in range(TOP_K):
            copy(r, k).start()
        return carry

    def wait(r, carry):
        for k in range(TOP_K):
            copy(r, k).wait()
        return carry

    lax.fori_loop(0, tb, start, 0, unroll=ROW_DMA_UNROLL)
    lax.fori_loop(0, tb, wait, 0, unroll=ROW_DMA_UNROLL)


def moe_dispatch(x, dest, n_rows, tpad, name):
    t, d = x.shape
    tb = _row_tile(t)
    grid_spec = pltpu.PrefetchScalarGridSpec(
        num_scalar_prefetch=1,
        grid=(t // tb,),
        in_specs=[pl.BlockSpec((tb, d), lambda i, dest: (i, 0)),
                  pl.BlockSpec(memory_space=pl.ANY)],
        out_specs=pl.BlockSpec(memory_space=pl.ANY),
        scratch_shapes=[pltpu.SemaphoreType.DMA(())],
    )
    return pl.pallas_call(
        functools.partial(_dispatch_body, tpad=tpad),
        grid_spec=grid_spec,
        out_shape=jax.ShapeDtypeStruct((n_rows, d), x.dtype),
        input_output_aliases={2: 0},
        compiler_params=_cparams(1),
        name=name,
    )(dest, x, jnp.zeros((n_rows, d), x.dtype))


def _experts_body(te_ref, nu_ref, x_ref, w1_ref, w3_ref, w2_ref, o_ref, xbf_ref):
    del te_ref
    i = pl.program_id(0)
    c = pl.program_id(1)

    @pl.when((i < nu_ref[0]) & (c == 0))
    def _():
        xbf_ref[...] = x_ref[...].astype(BF16)

    @pl.when(i < nu_ref[0])
    def _():
        xb = xbf_ref[...]
        h1 = jnp.dot(xb, w1_ref[...].astype(BF16), preferred_element_type=F32)
        h3 = jnp.dot(xb, w3_ref[...].astype(BF16), preferred_element_type=F32)
        hb = (h1 * jax.nn.sigmoid(h1) * h3).astype(BF16)
        y = jnp.dot(hb, w2_ref[...].astype(BF16), preferred_element_type=F32)

        @pl.when(c == 0)
        def _():
            o_ref[...] = y

        @pl.when(c > 0)
        def _():
            o_ref[...] += y

    @pl.when((i >= nu_ref[0]) & (c == 0))
    def _():
        o_ref[...] = jnp.zeros(o_ref.shape, o_ref.dtype)


def moe_experts(xbuf, tile_expert, n_used, w1, w3, w2, layer, name):
    n_rows, d = xbuf.shape
    de = w1.shape[-1]
    tm, tc = MOE_TILE, MOE_CHUNK
    n_tiles, n_chunks = n_rows // tm, de // tc

    def tile(i, nu):
        return jnp.minimum(i, nu[0] - 1)

    def chunk(i, c, nu):
        it = tile(i, nu)
        step_c = jnp.where(i < nu[0], c, n_chunks - 1)
        return jnp.where(it % 2 == 0, step_c, n_chunks - 1 - step_c)

    grid_spec = pltpu.PrefetchScalarGridSpec(
        num_scalar_prefetch=2,
        grid=(n_tiles, n_chunks),
        in_specs=[
            pl.BlockSpec((tm, d), lambda i, c, te, nu: (tile(i, nu), 0)),
            pl.BlockSpec((None, None, d, tc), lambda i, c, te, nu: (layer, te[tile(i, nu)], 0, chunk(i, c, nu))),
            pl.BlockSpec((None, None, d, tc), lambda i, c, te, nu: (layer, te[tile(i, nu)], 0, chunk(i, c, nu))),
            pl.BlockSpec((None, None, tc, d), lambda i, c, te, nu: (layer, te[tile(i, nu)], chunk(i, c, nu), 0)),
        ],
        out_specs=pl.BlockSpec((tm, d), lambda i, c, te, nu: (i, 0)),
        scratch_shapes=[pltpu.VMEM((tm, d), BF16)],
    )
    return pl.pallas_call(
        _experts_body,
        grid_spec=grid_spec,
        out_shape=jax.ShapeDtypeStruct((n_rows, d), F32),
        compiler_params=_cparams(2),
        name=name,
    )(tile_expert, n_used, xbuf, w1, w3, w2)


def _combine_body(dest_ref, x_ref, gate_ref, g_ref, b_ref, ybuf, o_ref, obf_ref, y0_ref, y1_ref, sem,
                  *, tpad, alpha):
    tb = x_ref.shape[0]
    t0 = pl.program_id(0) * tb
    bufs = (y0_ref, y1_ref)

    def copy(r, k):
        d = dest_ref[k * tpad + t0 + r]
        return pltpu.make_async_copy(ybuf.at[pl.ds(d, 1), :], bufs[k].at[pl.ds(r, 1), :], sem)

    def start(r, carry):
        for k in range(TOP_K):
            copy(r, k).start()
        return carry

    def wait(r, carry):
        for k in range(TOP_K):
            copy(r, k).wait()
        return carry

    lax.fori_loop(0, tb, start, 0, unroll=ROW_DMA_UNROLL)
    lax.fori_loop(0, tb, wait, 0, unroll=ROW_DMA_UNROLL)
    gate = gate_ref[...]
    moe = gate[:, 0:1] * y0_ref[...] + gate[:, 1:2] * y1_ref[...]
    y = _layer_norm_rows(alpha * x_ref[...] + moe, g_ref[...], b_ref[...])
    o_ref[...] = y
    obf_ref[...] = y.astype(BF16)


def moe_combine_ln(x, ybuf, dest, gate_cols, g, b, alpha, tpad, name):
    t, d = x.shape
    tb = _row_tile(t, 768)
    row = pl.BlockSpec((tb, d), lambda i, dest: (i, 0))
    vec = pl.BlockSpec((1, d), lambda i, dest: (0, 0))
    grid_spec = pltpu.PrefetchScalarGridSpec(
        num_scalar_prefetch=1,
        grid=(t // tb,),
        in_specs=[row, pl.BlockSpec((tb, TOP_K), lambda i, dest: (i, 0)), vec, vec,
                  pl.BlockSpec(memory_space=pl.ANY)],
        out_specs=[row, row],
        scratch_shapes=[pltpu.VMEM((tb, d), F32), pltpu.VMEM((tb, d), F32), pltpu.SemaphoreType.DMA(())],
    )
    return pl.pallas_call(
        functools.partial(_combine_body, tpad=tpad, alpha=alpha),
        grid_spec=grid_spec,
        out_shape=[jax.ShapeDtypeStruct((t, d), F32), jax.ShapeDtypeStruct((t, d), BF16)],
        compiler_params=_cparams(1),
        name=name,
    )(dest, x, gate_cols, g.reshape(1, d), b.reshape(1, d), ybuf)


def moe_layer(x, w_router, b_router, w1, w3, w2, layer, g, b, alpha):
    t, d = x.shape
    tm = MOE_TILE
    e_idx, gate, rank, counts = moe_route(x, w_router, b_router, name=f"moe_route_{layer}")
    tpad = e_idx.shape[1]
    counts = counts[:, 0]
    padded = (counts + tm - 1) // tm * tm
    pad_end = jnp.cumsum(padded)
    pad_start = pad_end - padded
    is_e = e_idx[:TOP_K, None, :] == jnp.arange(N_EXPERTS, dtype=I32)[None, :, None]
    dest = jnp.sum(jnp.where(is_e, pad_start[None, :, None], 0), axis=1) + rank[:TOP_K]
    dest = dest.reshape(-1).astype(I32)
    n_tiles = (t * TOP_K + N_EXPERTS * (tm - 1)) // tm + 1
    n_used = (pad_end[-1] // tm).astype(I32).reshape(1)
    tile_start = jnp.arange(n_tiles, dtype=I32) * tm
    tile_expert = jnp.minimum(jnp.sum((pad_end[None, :] <= tile_start[:, None]).astype(I32), axis=1), N_EXPERTS - 1)
    xbuf = moe_dispatch(x, dest, n_tiles * tm, tpad, name=f"moe_dispatch_{layer}")
    ybuf = moe_experts(xbuf, tile_expert, n_used, w1, w3, w2, layer, name=f"moe_experts_{layer}")
    gate_cols = gate[:TOP_K, :t].T
    return moe_combine_ln(x, ybuf, dest, gate_cols, g, b, alpha, tpad, name=f"moe_combine_{layer}")


def kernel(x_prompt, x_sample, state_mlstm_C, state_mlstm_n, state_mlstm_m, state_conv, cache_k, cache_v, page_table, w_in_rec, b_gate_i, b_gate_f, g_mlstm_norm, w_dw, b_dw, g_conv_norm, b_conv_norm, w_out_rec, w_qkv_attn, w_o_attn, rel_bias, ln_mix_g, ln_mix_b, ln_ffn_g, ln_ffn_b, w_router, b_router, w1_exp, w3_exp, w2_exp):
    bp, seq, d = x_prompt.shape
    db, sn, _ = x_sample.shape
    depth = ln_mix_g.shape[0]
    alpha = (2 * depth) ** 0.25
    tp, ts_ = bp * seq, db * sn
    heads_m, dv, dk = state_mlstm_C.shape[2:]
    ch = state_conv.shape[-1]
    qk_w = 2 * heads_m * dk + heads_m * dv
    n_gate = 2 * heads_m
    rest_w = heads_m * dv + 2 * ch
    assert w_in_rec.shape[-1] == qk_w + n_gate + rest_w and heads_m == M_HEADS
    assert tp % sn == 0 and tp % seq == 0
    dh = d // A_HEADS
    past = page_table.shape[1] * PAGE_SIZE
    ppb = MOBA_BLOCK // PAGE_SIZE
    n_full = past // MOBA_BLOCK
    assert past % MOBA_BLOCK == 0 and n_full >= 1 and sn <= MOBA_BLOCK, "sample keys of the current block must all be new"

    x = jnp.concatenate([x_prompt.reshape(tp, d), x_sample.reshape(ts_, d)], axis=0)
    x_in = x.astype(BF16)
    sdt = state_mlstm_C.dtype

    bias = bias_tiles(rel_bias)
    far = rel_bias[N_BUCKETS - 1].astype(F32)
    ksums = block_key_sums(cache_k, page_table, n_full, ppb)
    ksums = jnp.transpose(ksums, (2, 0, 3, 1, 4))

    w_rest = w_in_rec[:, :, qk_w + n_gate:]
    w_gate = jnp.pad(w_in_rec[:, :, qk_w:qk_w + n_gate], ((0, 0), (0, 0), (0, LANES - n_gate)))
    z_c = jnp.zeros((bp, heads_m, dv, dk), F32)
    z_n = jnp.zeros((bp, heads_m, dk), F32)
    z_m = jnp.zeros((bp, heads_m), F32)
    z_buf = jnp.zeros((bp, CONV_WIDTH - 1, ch), F32)

    t_all = tp + ts_
    p_tile = _row_tile(tp, 1024)
    assert tp % ts_ == 0
    rec_p, rec_s, kv_s = [], [], []
    kv_stacks = None
    for l in range(depth):
        if l % 2 == 0:
            r = l // 2
            u1 = matmul([x_in], w_in_rec, lead=r, n=qk_w, tn=PROJ_TN, out_dtype=F32, name=f"rec_in_qkv_{r}")
            u2 = matmul([x_in], w_rest, lead=r, n=rest_w, tn=PROJ_TN, out_dtype=F32, name=f"rec_in_rest_{r}")
            gates = matmul([x_in], w_gate, lead=r, n=LANES, tn=LANES, out_dtype=F32, name=f"rec_in_gates_{r}")
            b_g = jnp.concatenate([b_gate_i[r], b_gate_f[r]]).astype(F32)
            h_p, c_p, n_p, m_p = mlstm(u1, gates, u2, b_g, g_mlstm_norm[r], z_c, z_n, z_m,
                                       seq=seq, row_block0=0, out_dtype=BF16, name=f"mlstm_prompt_{r}")
            h_s, c_s, n_s, m_s = mlstm(u1, gates, u2, b_g, g_mlstm_norm[r], state_mlstm_C[r], state_mlstm_n[r],
                                       state_mlstm_m[r], seq=sn, row_block0=tp // sn, out_dtype=F32,
                                       name=f"mlstm_sample_{r}")
            conv_cols = (heads_m * dv) // ch
            assert (heads_m * dv) % ch == 0
            cv_p, buf_p = conv_module(u2, z_buf, w_dw[r], b_dw[r], g_conv_norm[r], b_conv_norm[r], seq=seq, ts=128,
                                      row_block0=0, col_block0=conv_cols, out_dtype=BF16, name=f"conv_prompt_{r}")
            cv_s, buf_s = conv_module(u2, state_conv[r].astype(F32), w_dw[r], b_dw[r], g_conv_norm[r], b_conv_norm[r],
                                      seq=sn, ts=sn, row_block0=tp // sn, col_block0=conv_cols, out_dtype=F32,
                                      name=f"conv_sample_{r}")
            f = matmul([h_p, cv_p], w_out_rec, lead=r, n=d, tn=PROJ_TN, out_dtype=F32, tm=p_tile, out_rows=t_all,
                       name=f"rec_out_{r}")
            f = matmul([h_s, cv_s], w_out_rec, lead=r, n=d, tn=PROJ_TN, out_dtype=F32, tm=ts_, out_rows=t_all,
                       row_block0=tp // ts_, out_init=f, name=f"rec_out_sample_{r}")
            rec_p.append((c_p, n_p, m_p, buf_p))
            rec_s.append((c_s, n_s, m_s, buf_s))
        else:
            a = l // 2
            qkv = matmul([x_in], w_qkv_attn, lead=a, n=3 * d, tn=PROJ_TN, out_dtype=F32, name=f"attn_qkv_{a}")
            o_p, k_stack, v_stack = moba_prompt(qkv, bias, far, batch=bp, seq=seq, layer=a, n_layers=depth // 2,
                                                kv_init=kv_stacks, name=f"moba_prompt_{a}")
            kv_stacks = (k_stack, v_stack)
            qkv_s = qkv[tp:].reshape(db, sn, 3, A_HEADS, dh).transpose(2, 0, 3, 1, 4)
            q_s, k_s, v_s = qkv_s[0], qkv_s[1], qkv_s[2]
            sel = sample_select(q_s, ksums[a])[..., :MOBA_TOPK]
            page_ids = sel[..., None] * ppb + jnp.arange(ppb, dtype=I32)
            pages = jnp.take_along_axis(page_table[:, None, None, :],
                                        page_ids.reshape(db, A_HEADS, sn, MOBA_TOPK * ppb), axis=-1)
            o_s = sample_attend(q_s, k_s, v_s, pages.reshape(-1).astype(I32), sel.reshape(-1), bias, far,
                                cache_k, cache_v, layer=a, n_full=n_full, ppb=ppb, name=f"moba_sample_{a}")
            o_s = o_s.transpose(0, 2, 1, 3).reshape(ts_, d)
            f = matmul([o_p], w_o_attn, lead=a, n=d, tn=PROJ_TN, out_dtype=F32, tm=p_tile, out_rows=t_all,
                       name=f"attn_out_{a}")
            f = matmul([o_s], w_o_attn, lead=a, n=d, tn=PROJ_TN, out_dtype=F32, tm=ts_, out_rows=t_all,
                       row_block0=tp // ts_, out_init=f, name=f"attn_out_sample_{a}")
            kv_s.append((k_s, v_s))
        x = ln_residual(x, f, ln_mix_g[l], ln_mix_b[l], alpha, name=f"ln_mix_{l}")
        x, x_in = moe_layer(x, w_router, b_router, w1_exp, w3_exp, w2_exp, l, ln_ffn_g[l], ln_ffn_b[l], alpha)

    pdt = x_prompt.dtype
    y_p = x[:tp].reshape(bp, seq, d)
    y_s = x[tp:].reshape(db, sn, d)
    return (y_p, y_s,
            jnp.stack([s[0] for s in rec_p]).astype(pdt), jnp.stack([s[1] for s in rec_p]).astype(pdt),
            jnp.stack([s[2] for s in rec_p]).astype(pdt), jnp.stack([s[3] for s in rec_p]),
            kv_stacks[0], kv_stacks[1],
            jnp.stack([s[0] for s in rec_s]).astype(sdt), jnp.stack([s[1] for s in rec_s]).astype(sdt),
            jnp.stack([s[2] for s in rec_s]).astype(sdt), jnp.stack([s[3] for s in rec_s]),
            jnp.stack([t[0] for t in kv_s]), jnp.stack([t[1] for t in kv_s]))
```
